```python
import math
import jax, jax.numpy as jnp
from jax import lax
import numpy as np

D_MODEL = 1024
BATCH = 8
SEQ = 4096
DEPTH = 2

SG_WIDTH = 2 * D_MODEL
SG_GROUPS = 8
SG_CHUNK = 128
DA_HEADS = D_MODEL // 128
DA_HEAD_DIM = 64
ROT_DIM = DA_HEAD_DIM // 4
ROPE_THETA = 500000.0
Q_BLOCK = 128
FFN_DENSE = 2816
N_EXPERTS = 8
TOP_K = 2
FFN_EXPERT = 3584
NORM_EPS = 1e-6

kernel_name = "hybrid_sgmlp_diffattn_moe"


def _rmsnorm(x, gain, eps=NORM_EPS):
    xf = x.astype(jnp.float32)
    y = xf * lax.rsqrt(jnp.mean(xf * xf, axis=-1, keepdims=True) + eps)
    return (y * gain.astype(jnp.float32)).astype(x.dtype)


def _swiglu(h, w_gate, w_up, w_down):
    return (jax.nn.silu(h @ w_gate) * (h @ w_up)) @ w_down


def _spatial_gating_mixer(h, w_in, v_gain, w_spatial, b_spatial, w_out):
    B, S, _ = h.shape
    z = jax.nn.gelu(h @ w_in)
    u, v = jnp.split(z, 2, axis=-1)
    v = _rmsnorm(v, v_gain)
    n_chunks = S // SG_CHUNK
    cg = SG_WIDTH // SG_GROUPS
    v = v.reshape(B, n_chunks, SG_CHUNK, SG_GROUPS, cg)
    causal = jnp.tril(jnp.ones((SG_CHUNK, SG_CHUNK), dtype=w_spatial.dtype))
    w_masked = w_spatial * causal[None]
    mixed = jnp.einsum('gts,bnsgc->bntgc', w_masked, v) + b_spatial.T[None, None, :, :, None]
    y = u * mixed.reshape(B, S, SG_WIDTH)
    return y @ w_out


def _rope_tables(positions):
    inv_freq = 1.0 / (ROPE_THETA ** (jnp.arange(0, ROT_DIM, 2, dtype=jnp.float32) / ROT_DIM))
    ang = positions.astype(jnp.float32)[..., None] * inv_freq
    return jnp.cos(ang), jnp.sin(ang)


def _partial_rope(x, cos, sin):
    half = ROT_DIM // 2
    xf = x.astype(jnp.float32)
    x1 = xf[..., :half]
    x2 = xf[..., half:ROT_DIM]
    out = jnp.concatenate([x1 * cos - x2 * sin, x2 * cos + x1 * sin, xf[..., ROT_DIM:]], axis=-1)
    return out.astype(x.dtype)


def _diff_attention(h, positions, w_qkv, q_gain, k_gain, lam_q1, lam_k1, lam_q2, lam_k2,
                    subln_gain, w_out, lambda_init):
    B, S, _ = h.shape
    H, Dh = DA_HEADS, DA_HEAD_DIM
    qkv = h @ w_qkv
    q, k, v = jnp.split(qkv, 3, axis=-1)
    q = _rmsnorm(q.reshape(B, S, H, 2, Dh), q_gain)
    k = _rmsnorm(k.reshape(B, S, H, 2, Dh), k_gain)
    v = v.reshape(B, S, H, 2 * Dh)
    cos, sin = _rope_tables(positions)
    cos = cos[:, :, None, None, :]
    sin = sin[:, :, None, None, :]
    q = _partial_rope(q, cos, sin)
    k = _partial_rope(k, cos, sin)

    f32 = jnp.float32
    lam = (jnp.exp(jnp.sum(lam_q1.astype(f32) * lam_k1.astype(f32)))
           - jnp.exp(jnp.sum(lam_q2.astype(f32) * lam_k2.astype(f32))) + lambda_init)
    scale = 1.0 / math.sqrt(Dh)

    n_blocks = S // Q_BLOCK
    k_t = k.transpose(0, 2, 3, 1, 4)
    v_t = v.transpose(0, 2, 1, 3)
    q_b = q.transpose(0, 2, 3, 1, 4).reshape(B, H, 2, n_blocks, Q_BLOCK, Dh)
    q_b = jnp.moveaxis(q_b, 3, 0)
    key_pos = jnp.arange(S)[None, :]

    def block(args):
        q_blk, blk = args
        s = jnp.einsum('bhcqd,bhckd->bhcqk', q_blk, k_t,
                       preferred_element_type=jnp.float32) * scale
        q_pos = blk * Q_BLOCK + jnp.arange(Q_BLOCK)[:, None]
        s = jnp.where(key_pos <= q_pos, s, -jnp.inf)
        p = jax.nn.softmax(s, axis=-1)
        a = (p[:, :, 0] - lam * p[:, :, 1]).astype(v_t.dtype)
        return jnp.einsum('bhqk,bhkd->bhqd', a, v_t)

    o = lax.map(block, (q_b, jnp.arange(n_blocks)))
    o = o.transpose(1, 0, 3, 2, 4).reshape(B, S, H, 2 * Dh)
    o = (_rmsnorm(o, subln_gain).astype(jnp.float32) * (1.0 - lambda_init)).astype(h.dtype)
    return o.reshape(B, S, H * 2 * Dh) @ w_out


def _moe_swiglu(h, w_router, w_gate, w_up, w_down):
    logits = (h @ w_router).astype(jnp.float32)
    top_vals, top_idx = lax.top_k(logits, TOP_K)
    top_w = jax.nn.softmax(top_vals, axis=-1)
    gates = jnp.sum(jax.nn.one_hot(top_idx, N_EXPERTS, dtype=jnp.float32) * top_w[..., None], axis=-2)
    out = jnp.zeros_like(h)
    for e in range(N_EXPERTS):
        y = _swiglu(h, w_gate[e], w_up[e], w_down[e])
        out = out + gates[..., e:e + 1].astype(h.dtype) * y
    return out


def setup_inputs(seed: int = 0) -> dict:
    key = jax.random.key(seed)
    ks = iter(jax.random.split(key, 40))
    f32 = jnp.float32

    def w(shape, fan_in):
        return jax.random.normal(next(ks), shape, f32) * (fan_in ** -0.5)

    def gain(shape):
        return 1.0 + 0.02 * jax.random.normal(next(ks), shape, f32)

    x = jax.random.normal(next(ks), (BATCH, SEQ, D_MODEL), f32)
    offset = jax.random.randint(next(ks), (BATCH, 1), 0, 1024, dtype=jnp.int32)
    positions = offset + jnp.arange(SEQ, dtype=jnp.int32)[None, :]
    qkv_width = 3 * DA_HEADS * 2 * DA_HEAD_DIM
    return {
        "x": x,
        "positions": positions,
        "l0_mix_norm": gain((D_MODEL,)),
        "l0_sg_w_in": w((D_MODEL, 2 * SG_WIDTH), D_MODEL),
        "l0_sg_v_norm": gain((SG_WIDTH,)),
        "l0_sg_w_spatial": w((SG_GROUPS, SG_CHUNK, SG_CHUNK), SG_CHUNK),
        "l0_sg_b_spatial": gain((SG_GROUPS, SG_CHUNK)),
        "l0_sg_w_out": w((SG_WIDTH, D_MODEL), SG_WIDTH),
        "l0_ffn_norm": gain((D_MODEL,)),
        "l0_ffn_w_gate": w((D_MODEL, FFN_DENSE), D_MODEL),
        "l0_ffn_w_up": w((D_MODEL, FFN_DENSE), D_MODEL),
        "l0_ffn_w_down": w((FFN_DENSE, D_MODEL), FFN_DENSE),
        "l1_mix_norm": gain((D_MODEL,)),
        "l1_da_w_qkv": w((D_MODEL, qkv_width), D_MODEL),
        "l1_da_q_norm": gain((DA_HEAD_DIM,)),
        "l1_da_k_norm": gain((DA_HEAD_DIM,)),
        "l1_da_lambda_q1": 0.1 * jax.random.normal(next(ks), (DA_HEAD_DIM,), f32),
        "l1_da_lambda_k1": 0.1 * jax.random.normal(next(ks), (DA_HEAD_DIM,), f32),
        "l1_da_lambda_q2": 0.1 * jax.random.normal(next(ks), (DA_HEAD_DIM,), f32),
        "l1_da_lambda_k2": 0.1 * jax.random.normal(next(ks), (DA_HEAD_DIM,), f32),
        "l1_da_subln": gain((2 * DA_HEAD_DIM,)),
        "l1_da_w_out": w((DA_HEADS * 2 * DA_HEAD_DIM, D_MODEL), DA_HEADS * 2 * DA_HEAD_DIM),
        "l1_moe_norm": gain((D_MODEL,)),
        "l1_moe_w_router": w((D_MODEL, N_EXPERTS), D_MODEL),
        "l1_moe_w_gate": w((N_EXPERTS, D_MODEL, FFN_EXPERT), D_MODEL),
        "l1_moe_w_up": w((N_EXPERTS, D_MODEL, FFN_EXPERT), D_MODEL),
        "l1_moe_w_down": w((N_EXPERTS, FFN_EXPERT, D_MODEL), FFN_EXPERT),
    }


def reference(x, positions,
              l0_mix_norm, l0_sg_w_in, l0_sg_v_norm, l0_sg_w_spatial, l0_sg_b_spatial, l0_sg_w_out,
              l0_ffn_norm, l0_ffn_w_gate, l0_ffn_w_up, l0_ffn_w_down,
              l1_mix_norm, l1_da_w_qkv, l1_da_q_norm, l1_da_k_norm,
              l1_da_lambda_q1, l1_da_lambda_k1, l1_da_lambda_q2, l1_da_lambda_k2,
              l1_da_subln, l1_da_w_out,
              l1_moe_norm, l1_moe_w_router, l1_moe_w_gate, l1_moe_w_up, l1_moe_w_down):
    mixers = [
        (l0_mix_norm, (l0_sg_w_in, l0_sg_v_norm, l0_sg_w_spatial, l0_sg_b_spatial, l0_sg_w_out)),
        (l1_mix_norm, (l1_da_w_qkv, l1_da_q_norm, l1_da_k_norm, l1_da_lambda_q1, l1_da_lambda_k1,
                       l1_da_lambda_q2, l1_da_lambda_k2, l1_da_subln, l1_da_w_out)),
    ]
    channel = [
        (l0_ffn_norm, (l0_ffn_w_gate, l0_ffn_w_up, l0_ffn_w_down)),
        (l1_moe_norm, (l1_moe_w_router, l1_moe_w_gate, l1_moe_w_up, l1_moe_w_down)),
    ]
    h = x
    for i in range(DEPTH):
        norm_g, p = mixers[i]
        hn = _rmsnorm(h, norm_g)
        if i % 2 == 0:
            h = h + _spatial_gating_mixer(hn, *p)
        else:
            lambda_init = 0.8 - 0.6 * math.exp(-0.3 * i)
            h = h + _diff_attention(hn, positions, *p[:-1], p[-1], lambda_init)
        norm_g, p = channel[i]
        hn = _rmsnorm(h, norm_g)
        if i % 2 == 0:
            h = h + _swiglu(hn, *p)
        else:
            h = h + _moe_swiglu(hn, *p)
    return h
```

```python
import math
from functools import partial

import jax
import jax.numpy as jnp
from jax import lax
from jax.experimental import pallas as pl
from jax.experimental.pallas import tpu as pltpu

F32 = jnp.float32
BF16 = jnp.bfloat16

D_MODEL = 1024
SG_WIDTH = 2 * D_MODEL
SG_GROUPS = 8
SG_CHUNK = 128
SG_GROUP_WIDTH = SG_WIDTH // SG_GROUPS
DA_HEADS = 8
DA_HEAD_DIM = 64
HEAD_WIDTH = 2 * DA_HEAD_DIM
ROT_DIM = DA_HEAD_DIM // 4
ROT_HALF = ROT_DIM // 2
ROPE_THETA = 500000.0
FFN_DENSE = 2816
N_EXPERTS = 8
FFN_EXPERT = 3584
NORM_EPS = 1e-6
LAMBDA_INIT = 0.8 - 0.6 * math.exp(-0.3 * 1)

LANES = 128
VMEM_LIMIT = 56 * 1024 * 1024

TM_SG = 512
TM_FFN = 512
TM_QKV = 512
TQ = 512
TK = 512
TM_PROJ = 512
TM_MOE = 1024
FFN_EXPERT_CHUNK = 512


def _rms(x, gain):
    return x * lax.rsqrt(jnp.mean(x * x, axis=-1, keepdims=True) + NORM_EPS) * gain


def _const_spec(shape):
    zeros = (0,) * len(shape)
    return pl.BlockSpec(shape, lambda *_: zeros, pipeline_mode=pl.Buffered(1))


def _params(*semantics):
    return pltpu.CompilerParams(dimension_semantics=semantics, vmem_limit_bytes=VMEM_LIMIT)


def _sg_mixer_kernel(x_ref, g_ref, win_ref, vg_ref, wsp_ref, bt_ref, wout_ref, o_ref, y_ref):
    x = x_ref[...]
    hn = _rms(x, g_ref[...]).astype(BF16)
    z = jax.nn.gelu(jnp.dot(hn, win_ref[...], preferred_element_type=F32))
    u = z[:, :SG_WIDTH]
    v = _rms(z[:, SG_WIDTH:], vg_ref[...]).astype(BF16)
    row = lax.broadcasted_iota(jnp.int32, (SG_CHUNK, SG_CHUNK), 0)
    col = lax.broadcasted_iota(jnp.int32, (SG_CHUNK, SG_CHUNK), 1)
    causal = col <= row
    for g in range(SG_GROUPS):
        w_masked = jnp.where(causal, wsp_ref[g], 0.0).astype(BF16)
        bias = bt_ref[:, g:g + 1]
        cols = slice(g * SG_GROUP_WIDTH, (g + 1) * SG_GROUP_WIDTH)
        for c in range(TM_SG // SG_CHUNK):
            rows = slice(c * SG_CHUNK, (c + 1) * SG_CHUNK)
            mixed = jnp.dot(w_masked, v[rows, cols], preferred_element_type=F32) + bias
            y_ref[rows, cols] = (u[rows, cols] * mixed).astype(BF16)
    o_ref[...] = x + jnp.dot(y_ref[...], wout_ref[...], preferred_element_type=F32)


def _sg_mixer(x, gain, w_in, v_gain, w_spatial, b_spatial_t, w_out):
    n = x.shape[0]
    tok = lambda i: (i, 0)
    return pl.pallas_call(
        _sg_mixer_kernel,
        grid=(n // TM_SG,),
        in_specs=[
            pl.BlockSpec((TM_SG, D_MODEL), tok),
            _const_spec((1, D_MODEL)),
            _const_spec((D_MODEL, 2 * SG_WIDTH)),
            _const_spec((1, SG_WIDTH)),
            _const_spec((SG_GROUPS, SG_CHUNK, SG_CHUNK)),
            _const_spec((SG_CHUNK, SG_GROUPS)),
            _const_spec((SG_WIDTH, D_MODEL)),
        ],
        out_specs=pl.BlockSpec((TM_SG, D_MODEL), tok),
        out_shape=jax.ShapeDtypeStruct((n, D_MODEL), F32),
        scratch_shapes=[pltpu.VMEM((TM_SG, SG_WIDTH), BF16)],
        compiler_params=_params("parallel"),
        name="sg_mixer",
    )(x, gain, w_in, v_gain, w_spatial, b_spatial_t, w_out)


def _dense_ffn_kernel(x_ref, g_ref, wg_ref, wu_ref, wd_ref, o_ref):
    x = x_ref[...]
    hn = _rms(x, g_ref[...]).astype(BF16)
    gate = jnp.dot(hn, wg_ref[...], preferred_element_type=F32)
    up = jnp.dot(hn, wu_ref[...], preferred_element_type=F32)
    act = (jax.nn.silu(gate) * up).astype(BF16)
    o_ref[...] = x + jnp.dot(act, wd_ref[...], preferred_element_type=F32)


def _dense_ffn(x, gain, w_gate, w_up, w_down):
    n = x.shape[0]
    tok = lambda i: (i, 0)
    return pl.pallas_call(
        _dense_ffn_kernel,
        grid=(n // TM_FFN,),
        in_specs=[
            pl.BlockSpec((TM_FFN, D_MODEL), tok),
            _const_spec((1, D_MODEL)),
            _const_spec((D_MODEL, FFN_DENSE)),
            _const_spec((D_MODEL, FFN_DENSE)),
            _const_spec((FFN_DENSE, D_MODEL)),
        ],
        out_specs=pl.BlockSpec((TM_FFN, D_MODEL), tok),
        out_shape=jax.ShapeDtypeStruct((n, D_MODEL), F32),
        compiler_params=_params("parallel"),
        name="dense_ffn",
    )(x, gain, w_gate, w_up, w_down)


def _qkv_kernel(x_ref, pos_ref, g_ref, w_ref, freq_ref, qg_ref, kg_ref, seg_ref,
                q_ref, k_ref, v_ref):
    hn = _rms(x_ref[...], g_ref[...]).astype(BF16)
    qkv = jnp.dot(hn, w_ref[...], preferred_element_type=F32)

    ang = pos_ref[...].astype(F32) * freq_ref[...]
    cos = jnp.cos(ang)
    sin = jnp.sin(ang)
    d = lax.broadcasted_iota(jnp.int32, (1, HEAD_WIDTH), 1) % DA_HEAD_DIM
    sin_lo = jnp.where(d < ROT_HALF, -sin, 0.0)
    sin_hi = jnp.where((d >= ROT_HALF) & (d < ROT_DIM), sin, 0.0)
    seg = seg_ref[...]

    def norm_rope(xh, gain):
        ssq = jnp.dot((xh * xh).astype(BF16), seg, preferred_element_type=F32)
        xn = xh * lax.rsqrt(ssq * (1.0 / DA_HEAD_DIM) + NORM_EPS) * gain
        nxt = pltpu.roll(xn, HEAD_WIDTH - ROT_HALF, 1)
        prv = pltpu.roll(xn, ROT_HALF, 1)
        return xn * cos + nxt * sin_lo + prv * sin_hi

    scale = 1.0 / math.sqrt(DA_HEAD_DIM)
    for h in range(DA_HEADS):
        cols = slice(h * HEAD_WIDTH, (h + 1) * HEAD_WIDTH)
        q = norm_rope(qkv[:, cols], qg_ref[...])
        q_ref[:, cols] = (q * scale).astype(BF16)
        kcols = slice(D_MODEL + h * HEAD_WIDTH, D_MODEL + (h + 1) * HEAD_WIDTH)
        k_ref[:, cols] = norm_rope(qkv[:, kcols], kg_ref[...]).astype(BF16)
    v_ref[...] = qkv[:, 2 * D_MODEL:].astype(BF16)


def _qkv(x, pos, gain, w_qkv, freq_lane, q_gain, k_gain, seg_ones):
    n = x.shape[0]
    tok = lambda i: (i, 0)
    out = jax.ShapeDtypeStruct((n, D_MODEL), BF16)
    return pl.pallas_call(
        _qkv_kernel,
        grid=(n // TM_QKV,),
        in_specs=[
            pl.BlockSpec((TM_QKV, D_MODEL), tok),
            pl.BlockSpec((TM_QKV, 1), tok),
            _const_spec((1, D_MODEL)),
            _const_spec((D_MODEL, 3 * D_MODEL)),
            _const_spec((1, HEAD_WIDTH)),
            _const_spec((1, HEAD_WIDTH)),
            _const_spec((1, HEAD_WIDTH)),
            _const_spec((HEAD_WIDTH, HEAD_WIDTH)),
        ],
        out_specs=[pl.BlockSpec((TM_QKV, D_MODEL), tok)] * 3,
        out_shape=[out, out, out],
        compiler_params=_params("parallel"),
        name="qkv_norm_rope",
    )(x, pos, gain, w_qkv, freq_lane, q_gain, k_gain, seg_ones)


def _attn_kernel(lam_ref, sub_ref, q_ref, k_ref, v_ref, o_ref, qq_ref, m_ref, l_ref, acc_ref):
    qi = pl.program_id(2)
    q = q_ref[0]
    lane = lax.broadcasted_iota(jnp.int32, (TQ, HEAD_WIDTH), 1)
    first = lane < DA_HEAD_DIM
    qq_ref[:TQ, :] = jnp.where(first, q, jnp.zeros_like(q))
    qq_ref[TQ:, :] = jnp.where(first, jnp.zeros_like(q), q)
    m_ref[...] = jnp.full(m_ref.shape, -jnp.inf, F32)
    l_ref[...] = jnp.zeros(l_ref.shape, F32)
    acc_ref[...] = jnp.zeros(acc_ref.shape, F32)

    def step(j, masked):
        k = k_ref[0, pl.ds(j * TK, TK), :]
        v = v_ref[0, pl.ds(j * TK, TK), :]
        s = lax.dot_general(qq_ref[...], k, (((1,), (1,)), ((), ())),
                            preferred_element_type=F32)
        if masked:
            row = lax.broadcasted_iota(jnp.int32, (2 * TQ, TK), 0) % TQ
            col = lax.broadcasted_iota(jnp.int32, (2 * TQ, TK), 1)
            s = jnp.where(col <= row, s, -jnp.inf)
        m_old = m_ref[...]
        m_new = jnp.maximum(m_old, jnp.max(s, axis=-1, keepdims=True))
        alpha = jnp.exp(m_old - m_new)
        p = jnp.exp(s - m_new)
        l_ref[...] = alpha * l_ref[...] + jnp.sum(p, axis=-1, keepdims=True)
        acc_ref[...] = alpha * acc_ref[...] + jnp.dot(p.astype(BF16), v,
                                                      preferred_element_type=F32)
        m_ref[...] = m_new

    def body(j, carry):
        step(j, False)
        return carry

    lax.fori_loop(0, qi, body, 0)
    step(qi, True)

    lam_terms = lam_ref[...]
    lam = (jnp.exp(jnp.sum(lam_terms[0:1] * lam_terms[1:2]))
           - jnp.exp(jnp.sum(lam_terms[2:3] * lam_terms[3:4])) + LAMBDA_INIT)
    o = acc_ref[:TQ, :] / l_ref[:TQ, :] - lam * (acc_ref[TQ:, :] / l_ref[TQ:, :])
    o_ref[0] = (_rms(o, sub_ref[...]) * (1.0 - LAMBDA_INIT)).astype(o_ref.dtype)


def _attention(q, k, v, lam_terms, subln, batch, seq):
    q = q.reshape(batch, seq, D_MODEL)
    k = k.reshape(batch, seq, D_MODEL)
    v = v.reshape(batch, seq, D_MODEL)
    q_spec = pl.BlockSpec((1, TQ, HEAD_WIDTH), lambda b, h, i: (b, i, h))
    kv_spec = pl.BlockSpec((1, seq, HEAD_WIDTH), lambda b, h, i: (b, 0, h))
    o = pl.pallas_call(
        _attn_kernel,
        grid=(batch, DA_HEADS, seq // TQ),
        in_specs=[
            _const_spec((4, DA_HEAD_DIM)),
            _const_spec((1, HEAD_WIDTH)),
            q_spec, kv_spec, kv_spec,
        ],
        out_specs=q_spec,
        out_shape=jax.ShapeDtypeStruct((batch, seq, D_MODEL), BF16),
        scratch_shapes=[
            pltpu.VMEM((2 * TQ, HEAD_WIDTH), BF16),
            pltpu.VMEM((2 * TQ, 1), F32),
            pltpu.VMEM((2 * TQ, 1), F32),
            pltpu.VMEM((2 * TQ, HEAD_WIDTH), F32),
        ],
        compiler_params=_params("parallel", "parallel", "arbitrary"),
        name="diff_attention",
    )(lam_terms, subln, q, k, v)
    return o.reshape(batch * seq, D_MODEL)


def _proj_route_kernel(o_ref, h_ref, wo_ref, g_ref, wr_ref, h2_ref, hn_ref, gates_ref):
    h2 = h_ref[...] + jnp.dot(o_ref[...], wo_ref[...], preferred_element_type=F32)
    h2_ref[...] = h2
    hn = _rms(h2, g_ref[...])
    hn_ref[...] = hn.astype(BF16)
    logits = jnp.dot(hn, wr_ref[...], preferred_element_type=F32,
                     precision=lax.Precision.HIGHEST)
    lane = lax.broadcasted_iota(jnp.int32, logits.shape, 1)
    logits = jnp.where(lane < N_EXPERTS, logits, -jnp.inf)
    m1 = jnp.max(logits, axis=-1, keepdims=True)
    i1 = jnp.min(jnp.where(logits == m1, lane, LANES), axis=-1, keepdims=True)
    rest = jnp.where(lane == i1, -jnp.inf, logits)
    m2 = jnp.max(rest, axis=-1, keepdims=True)
    i2 = jnp.min(jnp.where(rest == m2, lane, LANES), axis=-1, keepdims=True)
    e2 = jnp.exp(m2 - m1)
    denom = 1.0 + e2
    gates_ref[...] = (jnp.where(lane == i1, 1.0 / denom, 0.0)
                      + jnp.where(lane == i2, e2 / denom, 0.0))


def _proj_route(o, h, w_out, gain, w_router):
    n = h.shape[0]
    tok = lambda i: (i, 0)
    return pl.pallas_call(
        _proj_route_kernel,
        grid=(n // TM_PROJ,),
        in_specs=[
            pl.BlockSpec((TM_PROJ, D_MODEL), tok),
            pl.BlockSpec((TM_PROJ, D_MODEL), tok),
            _const_spec((D_MODEL, D_MODEL)),
            _const_spec((1, D_MODEL)),
            _const_spec((D_MODEL, LANES)),
        ],
        out_specs=[
            pl.BlockSpec((TM_PROJ, D_MODEL), tok),
            pl.BlockSpec((TM_PROJ, D_MODEL), tok),
            pl.BlockSpec((TM_PROJ, LANES), tok),
        ],
        out_shape=[
            jax.ShapeDtypeStruct((n, D_MODEL), F32),
            jax.ShapeDtypeStruct((n, D_MODEL), BF16),
            jax.ShapeDtypeStruct((n, LANES), F32),
        ],
        compiler_params=_params("parallel"),
        name="proj_route",
    )(o, h, w_out, gain, w_router)


def _moe_kernel(hn_ref, gates_ref, h_ref, wg_ref, wu_ref, wd_ref, o_ref, acc_ref):
    e = pl.program_id(1)
    j = pl.program_id(2)

    @pl.when((e == 0) & (j == 0))
    def _():
        acc_ref[...] = h_ref[...]

    hn = hn_ref[...]
    gate = jnp.dot(hn, wg_ref[0], preferred_element_type=F32)
    up = jnp.dot(hn, wu_ref[0], preferred_element_type=F32)
    act = (jax.nn.silu(gate) * up).astype(BF16)
    y = jnp.dot(act, wd_ref[0], preferred_element_type=F32)
    lane = lax.broadcasted_iota(jnp.int32, gates_ref.shape, 1)
    w = jnp.sum(jnp.where(lane == e, gates_ref[...], 0.0), axis=-1, keepdims=True)
    acc_ref[...] += w * y

    @pl.when((e == pl.num_programs(1) - 1) & (j == pl.num_programs(2) - 1))
    def _():
        o_ref[...] = acc_ref[...]


def _moe(hn, gates, h, w_gate, w_up, w_down):
    n = h.shape[0]
    tok = lambda i, e, j: (i, 0)
    return pl.pallas_call(
        _moe_kernel,
        grid=(n // TM_MOE, N_EXPERTS, FFN_EXPERT // FFN_EXPERT_CHUNK),
        in_specs=[
            pl.BlockSpec((TM_MOE, D_MODEL), tok),
            pl.BlockSpec((TM_MOE, LANES), tok),
            pl.BlockSpec((TM_MOE, D_MODEL), tok),
            pl.BlockSpec((1, D_MODEL, FFN_EXPERT_CHUNK), lambda i, e, j: (e, 0, j)),
            pl.BlockSpec((1, D_MODEL, FFN_EXPERT_CHUNK), lambda i, e, j: (e, 0, j)),
            pl.BlockSpec((1, FFN_EXPERT_CHUNK, D_MODEL), lambda i, e, j: (e, j, 0)),
        ],
        out_specs=pl.BlockSpec((TM_MOE, D_MODEL), tok),
        out_shape=jax.ShapeDtypeStruct((n, D_MODEL), F32),
        scratch_shapes=[pltpu.VMEM((TM_MOE, D_MODEL), F32)],
        compiler_params=_params("parallel", "arbitrary", "arbitrary"),
        name="moe_experts",
    )(hn, gates, h, w_gate, w_up, w_down)


def kernel(x, positions, l0_mix_norm, l0_sg_w_in, l0_sg_v_norm, l0_sg_w_spatial, l0_sg_b_spatial, l0_sg_w_out, l0_ffn_norm, l0_ffn_w_gate, l0_ffn_w_up, l0_ffn_w_down, l1_mix_norm, l1_da_w_qkv, l1_da_q_norm, l1_da_k_norm, l1_da_lambda_q1, l1_da_lambda_k1, l1_da_lambda_q2, l1_da_lambda_k2, l1_da_subln, l1_da_w_out, l1_moe_norm, l1_moe_w_router, l1_moe_w_gate, l1_moe_w_up, l1_moe_w_down):
    batch, seq, _ = x.shape
    n = batch * seq
    row = lambda a: a.reshape(1, -1)
    h = x.reshape(n, D_MODEL)

    h = _sg_mixer(h, row(l0_mix_norm), l0_sg_w_in.astype(BF16), row(l0_sg_v_norm),
                  l0_sg_w_spatial, l0_sg_b_spatial.T, l0_sg_w_out.astype(BF16))
    h = _dense_ffn(h, row(l0_ffn_norm), l0_ffn_w_gate.astype(BF16),
                   l0_ffn_w_up.astype(BF16), l0_ffn_w_down.astype(BF16))

    inv_freq = 1.0 / (ROPE_THETA ** (jnp.arange(0, ROT_DIM, 2, dtype=F32) / ROT_DIM))
    d = jnp.arange(HEAD_WIDTH) % DA_HEAD_DIM
    freq_lane = jnp.where(d < ROT_DIM, inv_freq[d % ROT_HALF], 0.0).reshape(1, HEAD_WIDTH)
    seg = jnp.arange(HEAD_WIDTH) // DA_HEAD_DIM
    seg_ones = (seg[:, None] == seg[None, :]).astype(BF16)
    q, k, v = _qkv(h, positions.reshape(n, 1), row(l1_mix_norm), l1_da_w_qkv.astype(BF16),
                   freq_lane, row(jnp.tile(l1_da_q_norm, 2)), row(jnp.tile(l1_da_k_norm, 2)),
                   seg_ones)
    lam_terms = jnp.stack([l1_da_lambda_q1, l1_da_lambda_k1, l1_da_lambda_q2, l1_da_lambda_k2])
    o = _attention(q, k, v, lam_terms, row(l1_da_subln), batch, seq)

    w_router = jnp.pad(l1_moe_w_router, ((0, 0), (0, LANES - N_EXPERTS)))
    h, hn, gates = _proj_route(o, h, l1_da_w_out.astype(BF16), row(l1_moe_norm), w_router)
    h = _moe(hn, gates, h, l1_moe_w_gate.astype(BF16), l1_moe_w_up.astype(BF16),
             l1_moe_w_down.astype(BF16))
    return h.reshape(batch, seq, D_MODEL)
```

```python
import math

import jax
import jax.numpy as jnp
from jax import lax
from jax.experimental import pallas as pl
from jax.experimental.pallas import tpu as pltpu

F32 = jnp.float32
BF16 = jnp.bfloat16
I32 = jnp.int32

D_MODEL = 1024
SG_WIDTH = 2 * D_MODEL
SG_GROUPS = 8
SG_CHUNK = 128
SG_GROUP_WIDTH = SG_WIDTH // SG_GROUPS
DA_HEADS = 8
DA_HEAD_DIM = 64
HEAD_WIDTH = 2 * DA_HEAD_DIM
ROT_DIM = DA_HEAD_DIM // 4
ROT_HALF = ROT_DIM // 2
ROPE_THETA = 500000.0
FFN_DENSE = 2816
N_EXPERTS = 8
TOP_K = 2
FFN_EXPERT = 3584
NORM_EPS = 1e-6
LAMBDA_INIT = 0.8 - 0.6 * math.exp(-0.3 * 1)

LANES = 128
VMEM_LIMIT = 56 * 1024 * 1024

TM_SG = 512
TM_FFN = 512
TK = 512
TQ = 512
QC = 256
TM_PROJ = 512
TM_GROUP = 1024
FFN_EXPERT_CHUNK = 512
TM_MOVE = 512
ROW_TILE = (8, LANES)
assert ROW_TILE[0] * ROW_TILE[1] == D_MODEL


def _store_rows(ref, x):
    for s in range(ROW_TILE[0]):
        ref[:, s, :] = x[:, s * LANES:(s + 1) * LANES]


def _rms(x, gain):
    return x * lax.rsqrt(jnp.mean(x * x, axis=-1, keepdims=True) + NORM_EPS) * gain


def _const_spec(shape):
    zeros = (0,) * len(shape)
    return pl.BlockSpec(shape, lambda *_: zeros, pipeline_mode=pl.Buffered(1))


def _params(*semantics):
    return pltpu.CompilerParams(dimension_semantics=semantics, vmem_limit_bytes=VMEM_LIMIT)


def _sg_mixer_kernel(x_ref, g_ref, win_ref, vg_ref, wsp_ref, bt_ref, wout_ref, o_ref, y_ref):
    x = x_ref[...]
    hn = _rms(x, g_ref[...]).astype(BF16)
    z = jax.nn.gelu(jnp.dot(hn, win_ref[...], preferred_element_type=F32))
    u = z[:, :SG_WIDTH]
    v = _rms(z[:, SG_WIDTH:], vg_ref[...]).astype(BF16)
    row = lax.broadcasted_iota(I32, (SG_CHUNK, SG_CHUNK), 0)
    col = lax.broadcasted_iota(I32, (SG_CHUNK, SG_CHUNK), 1)
    causal = col <= row
    for g in range(SG_GROUPS):
        w_masked = jnp.where(causal, wsp_ref[g], 0.0).astype(BF16)
        bias = bt_ref[:, g:g + 1]
        cols = slice(g * SG_GROUP_WIDTH, (g + 1) * SG_GROUP_WIDTH)
        for c in range(TM_SG // SG_CHUNK):
            rows = slice(c * SG_CHUNK, (c + 1) * SG_CHUNK)
            mixed = jnp.dot(w_masked, v[rows, cols], preferred_element_type=F32) + bias
            y_ref[rows, cols] = (u[rows, cols] * mixed).astype(BF16)
    o_ref[...] = x + jnp.dot(y_ref[...], wout_ref[...], preferred_element_type=F32)


def _sg_mixer(x, gain, w_in, v_gain, w_spatial, b_spatial_t, w_out):
    n = x.shape[0]
    tok = lambda i: (i, 0)
    return pl.pallas_call(
        _sg_mixer_kernel,
        grid=(n // TM_SG,),
        in_specs=[
            pl.BlockSpec((TM_SG, D_MODEL), tok),
            _const_spec((1, D_MODEL)),
            _const_spec((D_MODEL, 2 * SG_WIDTH)),
            _const_spec((1, SG_WIDTH)),
            _const_spec((SG_GROUPS, SG_CHUNK, SG_CHUNK)),
            _const_spec((SG_CHUNK, SG_GROUPS)),
            _const_spec((SG_WIDTH, D_MODEL)),
        ],
        out_specs=pl.BlockSpec((TM_SG, D_MODEL), tok),
        out_shape=jax.ShapeDtypeStruct((n, D_MODEL), F32),
        scratch_shapes=[pltpu.VMEM((TM_SG, SG_WIDTH), BF16)],
        compiler_params=_params("parallel"),
        name="sg_mixer",
    )(x, gain, w_in, v_gain, w_spatial, b_spatial_t, w_out)


def _dense_ffn_kernel(x_ref, g_ref, wg_ref, wu_ref, wd_ref, o_ref):
    x = x_ref[...]
    hn = _rms(x, g_ref[...]).astype(BF16)
    gate = jnp.dot(hn, wg_ref[...], preferred_element_type=F32)
    up = jnp.dot(hn, wu_ref[...], preferred_element_type=F32)
    act = (jax.nn.silu(gate) * up).astype(BF16)
    o_ref[...] = x + jnp.dot(act, wd_ref[...], preferred_element_type=F32)


def _dense_ffn(x, gain, w_gate, w_up, w_down):
    n = x.shape[0]
    tok = lambda i: (i, 0)
    return pl.pallas_call(
        _dense_ffn_kernel,
        grid=(n // TM_FFN,),
        in_specs=[
            pl.BlockSpec((TM_FFN, D_MODEL), tok),
            _const_spec((1, D_MODEL)),
            _const_spec((D_MODEL, FFN_DENSE)),
            _const_spec((D_MODEL, FFN_DENSE)),
            _const_spec((FFN_DENSE, D_MODEL)),
        ],
        out_specs=pl.BlockSpec((TM_FFN, D_MODEL), tok),
        out_shape=jax.ShapeDtypeStruct((n, D_MODEL), F32),
        compiler_params=_params("parallel"),
        name="dense_ffn",
    )(x, gain, w_gate, w_up, w_down)


def _qkv_kernel(x_ref, pos_ref, g_ref, w_ref, freq_ref, qg_ref, kg_ref, seg_ref,
                q_ref, k_ref, vt_ref):
    hn = _rms(x_ref[...], g_ref[...]).astype(BF16)
    qkv = jnp.dot(hn, w_ref[...], preferred_element_type=F32)

    ang = pos_ref[...].astype(F32) * freq_ref[...]
    cos = jnp.cos(ang)
    sin = jnp.sin(ang)
    d = lax.broadcasted_iota(I32, (1, HEAD_WIDTH), 1) % DA_HEAD_DIM
    sin_lo = jnp.where(d < ROT_HALF, -sin, 0.0)
    sin_hi = jnp.where((d >= ROT_HALF) & (d < ROT_DIM), sin, 0.0)
    seg = seg_ref[...]

    def norm_rope(xh, gain):
        ssq = jnp.dot((xh * xh).astype(BF16), seg, preferred_element_type=F32)
        xn = xh * lax.rsqrt(ssq * (1.0 / DA_HEAD_DIM) + NORM_EPS) * gain
        nxt = pltpu.roll(xn, HEAD_WIDTH - ROT_HALF, 1)
        prv = pltpu.roll(xn, ROT_HALF, 1)
        return xn * cos + nxt * sin_lo + prv * sin_hi

    scale = 1.0 / math.sqrt(DA_HEAD_DIM)
    for h in range(DA_HEADS):
        cols = slice(h * HEAD_WIDTH, (h + 1) * HEAD_WIDTH)
        q = norm_rope(qkv[:, cols], qg_ref[...])
        q_ref[:, cols] = (q * scale).astype(BF16)
        kcols = slice(D_MODEL + h * HEAD_WIDTH, D_MODEL + (h + 1) * HEAD_WIDTH)
        k_ref[:, cols] = norm_rope(qkv[:, kcols], kg_ref[...]).astype(BF16)
        vcols = slice(2 * D_MODEL + h * HEAD_WIDTH, 2 * D_MODEL + (h + 1) * HEAD_WIDTH)
        vt_ref[0, h, 0] = qkv[:, vcols].T.astype(BF16)


def _qkv(x, pos, gain, w_qkv, freq_lane, q_gain, k_gain, seg_ones, batch, seq):
    n = x.shape[0]
    tiles = seq // TK
    tok = lambda i: (i, 0)
    out = jax.ShapeDtypeStruct((n, D_MODEL), BF16)
    return pl.pallas_call(
        _qkv_kernel,
        grid=(n // TK,),
        in_specs=[
            pl.BlockSpec((TK, D_MODEL), tok),
            pl.BlockSpec((TK, 1), tok),
            _const_spec((1, D_MODEL)),
            _const_spec((D_MODEL, 3 * D_MODEL)),
            _const_spec((1, HEAD_WIDTH)),
            _const_spec((1, HEAD_WIDTH)),
            _const_spec((1, HEAD_WIDTH)),
            _const_spec((HEAD_WIDTH, HEAD_WIDTH)),
        ],
        out_specs=[
            pl.BlockSpec((TK, D_MODEL), tok),
            pl.BlockSpec((TK, D_MODEL), tok),
            pl.BlockSpec((1, DA_HEADS, 1, HEAD_WIDTH, TK),
                         lambda i: (i // tiles, 0, i % tiles, 0, 0)),
        ],
        out_shape=[out, out,
                   jax.ShapeDtypeStruct((batch, DA_HEADS, tiles, HEAD_WIDTH, TK), BF16)],
        compiler_params=_params("parallel"),
        name="qkv_norm_rope",
    )(x, pos, gain, w_qkv, freq_lane, q_gain, k_gain, seg_ones)


def _attn_kernel(lam_ref, sub_ref, q_ref, k_ref, vt_ref, o_ref, qqt_ref, m_ref, l_ref, acc_ref):
    qi = pl.program_id(2)
    qt = q_ref[0].astype(F32).T
    first = lax.broadcasted_iota(I32, (HEAD_WIDTH, TQ), 0) < DA_HEAD_DIM
    qqt_ref[:, :TQ] = jnp.where(first, qt, 0.0).astype(BF16)
    qqt_ref[:, TQ:] = jnp.where(first, 0.0, qt).astype(BF16)
    m_ref[...] = jnp.full(m_ref.shape, -jnp.inf, F32)
    l_ref[...] = jnp.zeros(l_ref.shape, F32)
    acc_ref[...] = jnp.zeros(acc_ref.shape, F32)

    n_chunks = 2 * TQ // QC
    chunk_cols = [slice(c * QC, (c + 1) * QC) for c in range(n_chunks)]

    def scores(k, cols):
        return jnp.dot(k, qqt_ref[:, cols], preferred_element_type=F32)

    def fold(s, cols, vt, q_offset):
        if q_offset is not None:
            key = lax.broadcasted_iota(I32, s.shape, 0)
            qry = lax.broadcasted_iota(I32, s.shape, 1) + q_offset
            s = jnp.where(key <= qry, s, -jnp.inf)
        m_old = m_ref[:, cols]
        m_new = jnp.maximum(m_old, jnp.max(s, axis=0, keepdims=True))
        alpha = jnp.exp(m_old - m_new)
        p = jnp.exp(s - m_new)
        l_ref[:, cols] = alpha * l_ref[:, cols] + jnp.sum(p, axis=0, keepdims=True)
        acc_ref[:, cols] = alpha * acc_ref[:, cols] + jnp.dot(
            vt, p.astype(BF16), preferred_element_type=F32)
        m_ref[:, cols] = m_new

    def block(k, vt, nkeys, q_offsets):
        s_next = scores(k[:nkeys[0]], chunk_cols[0])
        for c in range(n_chunks):
            s = s_next
            if c + 1 < n_chunks:
                s_next = scores(k[:nkeys[c + 1]], chunk_cols[c + 1])
            fold(s, chunk_cols[c], vt[:, :nkeys[c]], q_offsets[c])

    def full_block(j, carry):
        k = k_ref[0, pl.ds(pl.multiple_of(j * TK, TK), TK), :]
        block(k, vt_ref[0, 0, j], [TK] * n_chunks, [None] * n_chunks)
        return carry

    lax.fori_loop(0, qi, full_block, 0)

    k_diag = k_ref[0, pl.ds(pl.multiple_of(qi * TK, TK), TK), :]
    q_offsets = [(c * QC) % TQ for c in range(n_chunks)]
    block(k_diag, vt_ref[0, 0, qi], [q0 + QC for q0 in q_offsets], q_offsets)

    lam_terms = lam_ref[...]
    lam = (jnp.exp(jnp.sum(lam_terms[0:1] * lam_terms[1:2]))
           - jnp.exp(jnp.sum(lam_terms[2:3] * lam_terms[3:4])) + LAMBDA_INIT)
    ot = (acc_ref[:, :TQ] / l_ref[:, :TQ] - lam * (acc_ref[:, TQ:] / l_ref[:, TQ:]))
    o_ref[0] = (_rms(ot.T, sub_ref[...]) * (1.0 - LAMBDA_INIT)).astype(o_ref.dtype)


def _attention(q, k, vt, lam_terms, subln, batch, seq):
    q = q.reshape(batch, seq, D_MODEL)
    k = k.reshape(batch, seq, D_MODEL)
    q_spec = pl.BlockSpec((1, TQ, HEAD_WIDTH), lambda b, h, i: (b, i, h))
    o = pl.pallas_call(
        _attn_kernel,
        grid=(batch, DA_HEADS, seq // TQ),
        in_specs=[
            _const_spec((4, DA_HEAD_DIM)),
            _const_spec((1, HEAD_WIDTH)),
            q_spec,
            pl.BlockSpec((1, seq, HEAD_WIDTH), lambda b, h, i: (b, 0, h)),
            pl.BlockSpec((1, 1, seq // TK, HEAD_WIDTH, TK), lambda b, h, i: (b, h, 0, 0, 0)),
        ],
        out_specs=q_spec,
        out_shape=jax.ShapeDtypeStruct((batch, seq, D_MODEL), BF16),
        scratch_shapes=[
            pltpu.VMEM((HEAD_WIDTH, 2 * TQ), BF16),
            pltpu.VMEM((1, 2 * TQ), F32),
            pltpu.VMEM((1, 2 * TQ), F32),
            pltpu.VMEM((HEAD_WIDTH, 2 * TQ), F32),
        ],
        compiler_params=_params("parallel", "parallel", "arbitrary"),
        name="diff_attention",
    )(lam_terms, subln, q, k, vt)
    return o.reshape(batch * seq, D_MODEL)


def _proj_route_kernel(o_ref, h_ref, wo_ref, g_ref, wr_ref,
                       h2_ref, hn_ref, ri_ref, rw_ref, cnt_ref, run_ref):
    @pl.when(pl.program_id(0) == 0)
    def _():
        run_ref[...] = jnp.zeros(run_ref.shape, F32)

    h2 = h_ref[...] + jnp.dot(o_ref[...], wo_ref[...], preferred_element_type=F32)
    h2_ref[...] = h2
    hn = _rms(h2, g_ref[...])
    _store_rows(hn_ref, hn)
    logits = jnp.dot(hn, wr_ref[...], preferred_element_type=F32,
                     precision=lax.Precision.HIGHEST)
    lane = lax.broadcasted_iota(I32, logits.shape, 1)
    logits = jnp.where(lane < N_EXPERTS, logits, -jnp.inf)
    m1 = jnp.max(logits, axis=-1, keepdims=True)
    i1 = jnp.min(jnp.where(logits == m1, lane, LANES), axis=-1, keepdims=True)
    rest = jnp.where(lane == i1, -jnp.inf, logits)
    m2 = jnp.max(rest, axis=-1, keepdims=True)
    i2 = jnp.min(jnp.where(rest == m2, lane, LANES), axis=-1, keepdims=True)
    e2 = jnp.exp(m2 - m1)
    denom = 1.0 + e2
    rw_ref[...] = jnp.where(lane == 0, 1.0 / denom, jnp.where(lane == 1, e2 / denom, 0.0))

    sel1 = lane == i1
    sel2 = lane == i2
    sel = jnp.where(sel1 | sel2, 1.0, 0.0)
    t_row = lax.broadcasted_iota(I32, (TM_PROJ, TM_PROJ), 0)
    t_col = lax.broadcasted_iota(I32, (TM_PROJ, TM_PROJ), 1)
    earlier = jnp.where(t_col < t_row, 1.0, 0.0).astype(BF16)
    rank = run_ref[...] + jnp.dot(earlier, sel.astype(BF16), preferred_element_type=F32)
    rank1 = jnp.sum(jnp.where(sel1, rank, 0.0), axis=-1, keepdims=True).astype(I32)
    rank2 = jnp.sum(jnp.where(sel2, rank, 0.0), axis=-1, keepdims=True).astype(I32)
    ri_ref[...] = jnp.where(lane == 0, i1, jnp.where(lane == 1, i2,
                            jnp.where(lane == 2, rank1, jnp.where(lane == 3, rank2, 0))))
    run = run_ref[...] + jnp.sum(sel, axis=0, keepdims=True)
    run_ref[...] = run
    cnt_ref[...] = jnp.broadcast_to(run, cnt_ref.shape)


def _proj_route(o, h, w_out, gain, w_router):
    n = h.shape[0]
    tok = lambda i: (i, 0)
    return pl.pallas_call(
        _proj_route_kernel,
        grid=(n // TM_PROJ,),
        in_specs=[
            pl.BlockSpec((TM_PROJ, D_MODEL), tok),
            pl.BlockSpec((TM_PROJ, D_MODEL), tok),
            _const_spec((D_MODEL, D_MODEL)),
            _const_spec((1, D_MODEL)),
            _const_spec((D_MODEL, LANES)),
        ],
        out_specs=[
            pl.BlockSpec((TM_PROJ, D_MODEL), tok),
            pl.BlockSpec((TM_PROJ,) + ROW_TILE, lambda i: (i, 0, 0)),
            pl.BlockSpec((TM_PROJ, LANES), tok),
            pl.BlockSpec((TM_PROJ, LANES), tok),
            pl.BlockSpec((8, LANES), lambda i: (0, 0)),
        ],
        out_shape=[
            jax.ShapeDtypeStruct((n, D_MODEL), F32),
            jax.ShapeDtypeStruct((n,) + ROW_TILE, F32),
            jax.ShapeDtypeStruct((n, LANES), I32),
            jax.ShapeDtypeStruct((n, LANES), F32),
            jax.ShapeDtypeStruct((8, LANES), F32),
        ],
        scratch_shapes=[pltpu.VMEM((1, LANES), F32)],
        compiler_params=_params("arbitrary"),
        name="proj_route",
    )(o, h, w_out, gain, w_router)


def _row_copy(src_ref, src_row, dst_ref, dst_row, sem):
    return pltpu.make_async_copy(src_ref.at[pl.ds(src_row, 1)], dst_ref.at[pl.ds(dst_row, 1)], sem)


def _dispatch_kernel(d0_ref, d1_ref, zrow_ref, hn_ref, xs_ref, zero_ref, zsem, sem):
    i = pl.program_id(0)

    def zero_fill(z):
        return pltpu.make_async_copy(zero_ref, xs_ref.at[pl.ds(zrow_ref[z], TM_GROUP)], zsem)

    @pl.when(i == 0)
    def _():
        zero_ref[...] = jnp.zeros(zero_ref.shape, F32)
        for z in range(2 * N_EXPERTS):
            @pl.when(zrow_ref[z] >= 0)
            def _():
                zero_fill(z).start()
        for z in range(2 * N_EXPERTS):
            @pl.when(zrow_ref[z] >= 0)
            def _():
                zero_fill(z).wait()

    base = i * TM_MOVE

    def issue(t, carry):
        _row_copy(hn_ref, base + t, xs_ref, d0_ref[base + t], sem).start()
        _row_copy(hn_ref, base + t, xs_ref, d1_ref[base + t], sem).start()
        return carry

    lax.fori_loop(0, TM_MOVE, issue, 0, unroll=8)

    def drain(t, carry):
        _row_copy(hn_ref, base + t, xs_ref, d0_ref[base + t], sem).wait()
        _row_copy(hn_ref, base + t, xs_ref, d1_ref[base + t], sem).wait()
        return carry

    lax.fori_loop(0, TM_MOVE, drain, 0, unroll=8)


def _dispatch(d0, d1, zero_rows, hn, rows):
    n = hn.shape[0]
    return pl.pallas_call(
        _dispatch_kernel,
        grid_spec=pltpu.PrefetchScalarGridSpec(
            num_scalar_prefetch=3,
            grid=(n // TM_MOVE,),
            in_specs=[pl.BlockSpec(memory_space=pl.ANY)],
            out_specs=pl.BlockSpec(memory_space=pl.ANY),
            scratch_shapes=[
                pltpu.VMEM((TM_GROUP,) + ROW_TILE, F32),
                pltpu.SemaphoreType.DMA,
                pltpu.SemaphoreType.DMA,
            ],
        ),
        out_shape=jax.ShapeDtypeStruct((rows,) + ROW_TILE, F32),
        compiler_params=_params("arbitrary"),
        name="moe_dispatch",
    )(d0, d1, zero_rows, hn)


def _group_ffn_kernel(te_ref, tv_ref, x_ref, wg_ref, wu_ref, wd_ref, o_ref, xb_ref, acc_ref):
    g = pl.program_id(0)
    j = pl.program_id(1)

    @pl.when((tv_ref[g] == 0) & (j == 0))
    def _():
        o_ref[...] = jnp.zeros(o_ref.shape, F32)

    @pl.when(tv_ref[g] > 0)
    def _():
        @pl.when(j == 0)
        def _():
            for s in range(ROW_TILE[0]):
                xb_ref[:, s * LANES:(s + 1) * LANES] = x_ref[:, s, :].astype(BF16)

        x = xb_ref[...]
        gate = jnp.dot(x, wg_ref[0], preferred_element_type=F32)
        up = jnp.dot(x, wu_ref[0], preferred_element_type=F32)
        act = (jax.nn.silu(gate) * up).astype(BF16)
        y = jnp.dot(act, wd_ref[0], preferred_element_type=F32)

        @pl.when(j == 0)
        def _():
            acc_ref[...] = y

        @pl.when(j > 0)
        def _():
            acc_ref[...] += y

        @pl.when(j == pl.num_programs(1) - 1)
        def _():
            _store_rows(o_ref, acc_ref[...])


def _group_ffn(tile_expert, tile_valid, xs, w_gate, w_up, w_down):
    rows = xs.shape[0]
    n_chunks = FFN_EXPERT // FFN_EXPERT_CHUNK
    last = n_chunks - 1
    chunk = lambda g, j, tv: jnp.where(tv[g] > 0, j, last)
    row_map = lambda g, j, te, tv: (g, 0, 0)
    return pl.pallas_call(
        _group_ffn_kernel,
        grid_spec=pltpu.PrefetchScalarGridSpec(
            num_scalar_prefetch=2,
            grid=(rows // TM_GROUP, n_chunks),
            in_specs=[
                pl.BlockSpec((TM_GROUP,) + ROW_TILE, row_map),
                pl.BlockSpec((1, D_MODEL, FFN_EXPERT_CHUNK),
                             lambda g, j, te, tv: (te[g], 0, chunk(g, j, tv))),
                pl.BlockSpec((1, D_MODEL, FFN_EXPERT_CHUNK),
                             lambda g, j, te, tv: (te[g], 0, chunk(g, j, tv))),
                pl.BlockSpec((1, FFN_EXPERT_CHUNK, D_MODEL),
                             lambda g, j, te, tv: (te[g], chunk(g, j, tv), 0)),
            ],
            out_specs=pl.BlockSpec((TM_GROUP,) + ROW_TILE, row_map),
            scratch_shapes=[
                pltpu.VMEM((TM_GROUP, D_MODEL), BF16),
                pltpu.VMEM((TM_GROUP, D_MODEL), F32),
            ],
        ),
        out_shape=jax.ShapeDtypeStruct((rows,) + ROW_TILE, F32),
        compiler_params=_params("arbitrary", "arbitrary"),
        name="moe_group_ffn",
    )(tile_expert, tile_valid, xs, w_gate, w_up, w_down)


def _combine_kernel(d0_ref, d1_ref, h_ref, rw_ref, ys_ref, o_ref, r0_ref, r1_ref, sem):
    base = pl.program_id(0) * TM_MOVE

    def issue(t, carry):
        _row_copy(ys_ref, d0_ref[base + t], r0_ref, t, sem).start()
        _row_copy(ys_ref, d1_ref[base + t], r1_ref, t, sem).start()
        return carry

    lax.fori_loop(0, TM_MOVE, issue, 0, unroll=8)

    def drain(t, carry):
        _row_copy(ys_ref, d0_ref[base + t], r0_ref, t, sem).wait()
        _row_copy(ys_ref, d1_ref[base + t], r1_ref, t, sem).wait()
        return carry

    lax.fori_loop(0, TM_MOVE, drain, 0, unroll=8)
    rw = rw_ref[...]
    w1 = rw[:, 0:1]
    w2 = rw[:, 1:2]
    for s in range(ROW_TILE[0]):
        cols = slice(s * LANES, (s + 1) * LANES)
        o_ref[:, cols] = h_ref[:, cols] + w1 * r0_ref[:, s, :] + w2 * r1_ref[:, s, :]


def _combine(d0, d1, h, route_w, ys):
    n = h.shape[0]
    tok = lambda i, d0, d1: (i, 0)
    return pl.pallas_call(
        _combine_kernel,
        grid_spec=pltpu.PrefetchScalarGridSpec(
            num_scalar_prefetch=2,
            grid=(n // TM_MOVE,),
            in_specs=[
                pl.BlockSpec((TM_MOVE, D_MODEL), tok),
                pl.BlockSpec((TM_MOVE, LANES), tok),
                pl.BlockSpec(memory_space=pl.ANY),
            ],
            out_specs=pl.BlockSpec((TM_MOVE, D_MODEL), tok),
            scratch_shapes=[
                pltpu.VMEM((TM_MOVE,) + ROW_TILE, F32),
                pltpu.VMEM((TM_MOVE,) + ROW_TILE, F32),
                pltpu.SemaphoreType.DMA,
            ],
        ),
        out_shape=jax.ShapeDtypeStruct((n, D_MODEL), F32),
        compiler_params=_params("arbitrary"),
        name="moe_combine",
    )(d0, d1, h, route_w, ys)


def _routing_tables(route_i, counts, n):
    counts = counts.astype(I32)
    padded = (counts + TM_GROUP - 1) // TM_GROUP * TM_GROUP
    ends = jnp.cumsum(padded)
    starts = ends - padded
    d0 = starts[route_i[:, 0]] + route_i[:, 2]
    d1 = starts[route_i[:, 1]] + route_i[:, 3]
    n_tiles = TOP_K * n // TM_GROUP + N_EXPERTS
    used = ends[-1] // TM_GROUP
    g = jnp.arange(n_tiles, dtype=I32)
    tile_expert = jnp.minimum(
        jnp.searchsorted(ends, jnp.minimum(g, used - 1) * TM_GROUP, side="right"),
        N_EXPERTS - 1).astype(I32)
    tile_valid = (g < used).astype(I32)
    tail = g[-N_EXPERTS:]
    zero_rows = jnp.concatenate([
        jnp.where(padded > 0, ends - TM_GROUP, -1),
        jnp.where(tail >= used, tail * TM_GROUP, -1)]).astype(I32)
    return d0, d1, zero_rows, tile_expert, tile_valid, n_tiles * TM_GROUP


def kernel(x, positions, l0_mix_norm, l0_sg_w_in, l0_sg_v_norm, l0_sg_w_spatial, l0_sg_b_spatial, l0_sg_w_out, l0_ffn_norm, l0_ffn_w_gate, l0_ffn_w_up, l0_ffn_w_down, l1_mix_norm, l1_da_w_qkv, l1_da_q_norm, l1_da_k_norm, l1_da_lambda_q1, l1_da_lambda_k1, l1_da_lambda_q2, l1_da_lambda_k2, l1_da_subln, l1_da_w_out, l1_moe_norm, l1_moe_w_router, l1_moe_w_gate, l1_moe_w_up, l1_moe_w_down):
    batch, seq, _ = x.shape
    n = batch * seq
    row = lambda a: a.reshape(1, -1)
    h = x.reshape(n, D_MODEL)

    h = _sg_mixer(h, row(l0_mix_norm), l0_sg_w_in.astype(BF16), row(l0_sg_v_norm),
                  l0_sg_w_spatial, l0_sg_b_spatial.T, l0_sg_w_out.astype(BF16))
    h = _dense_ffn(h, row(l0_ffn_norm), l0_ffn_w_gate.astype(BF16),
                   l0_ffn_w_up.astype(BF16), l0_ffn_w_down.astype(BF16))

    inv_freq = 1.0 / (ROPE_THETA ** (jnp.arange(0, ROT_DIM, 2, dtype=F32) / ROT_DIM))
    d = jnp.arange(HEAD_WIDTH) % DA_HEAD_DIM
    freq_lane = jnp.where(d < ROT_DIM, inv_freq[d % ROT_HALF], 0.0).reshape(1, HEAD_WIDTH)
    seg = jnp.arange(HEAD_WIDTH) // DA_HEAD_DIM
    seg_ones = (seg[:, None] == seg[None, :]).astype(BF16)
    q, k, vt = _qkv(h, positions.reshape(n, 1), row(l1_mix_norm), l1_da_w_qkv.astype(BF16),
                    freq_lane, row(jnp.tile(l1_da_q_norm, 2)), row(jnp.tile(l1_da_k_norm, 2)),
                    seg_ones, batch, seq)
    lam_terms = jnp.stack([l1_da_lambda_q1, l1_da_lambda_k1, l1_da_lambda_q2, l1_da_lambda_k2])
    o = _attention(q, k, vt, lam_terms, row(l1_da_subln), batch, seq)

    w_router = jnp.pad(l1_moe_w_router, ((0, 0), (0, LANES - N_EXPERTS)))
    h, hn, route_i, route_w, counts = _proj_route(
        o, h, l1_da_w_out.astype(BF16), row(l1_moe_norm), w_router)
    d0, d1, zero_rows, tile_expert, tile_valid, rows = _routing_tables(
        route_i, counts[0, :N_EXPERTS], n)
    xs = _dispatch(d0, d1, zero_rows, hn, rows)
    ys = _group_ffn(tile_expert, tile_valid, xs, l1_moe_w_gate.astype(BF16),
                    l1_moe_w_up.astype(BF16), l1_moe_w_down.astype(BF16))
    h = _combine(d0, d1, h, route_w, ys)
    return h.reshape(batch, seq, D_MODEL)
```

```python
import math

import jax
import jax.numpy as jnp
from jax import lax
from jax.experimental import pallas as pl
from jax.experimental.pallas import tpu as pltpu

F32 = jnp.float32
BF16 = jnp.bfloat16
I32 = jnp.int32

D_MODEL = 1024
SG_WIDTH = 2 * D_MODEL
SG_GROUPS = 8
SG_CHUNK = 128
SG_GROUP_WIDTH = SG_WIDTH // SG_GROUPS
DA_HEADS = 8
DA_HEAD_DIM = 64
HEAD_WIDTH = 2 * DA_HEAD_DIM
ROT_DIM = DA_HEAD_DIM // 4
ROT_HALF = ROT_DIM // 2
ROPE_THETA = 500000.0
FFN_DENSE = 2816
N_EXPERTS = 8
TOP_K = 2
FFN_EXPERT = 3584
NORM_EPS = 1e-6
LAMBDA_INIT = 0.8 - 0.6 * math.exp(-0.3 * 1)

LANES = 128
VMEM_LIMIT = 56 * 1024 * 1024

TM_SG = 512
TM_FFN = 512
TK = 512
TQ = 512
QC = 256
TM_PROJ = 512
TM_GROUP = 1024
FFN_EXPERT_CHUNK = 512
TM_MOVE = 512
ROW_TILE = (8, LANES)
assert ROW_TILE[0] * ROW_TILE[1] == D_MODEL


def _store_rows(ref, x):
    for s in range(ROW_TILE[0]):
        ref[:, s, :] = x[:, s * LANES:(s + 1) * LANES]


def _rms(x, gain):
    return x * lax.rsqrt(jnp.mean(x * x, axis=-1, keepdims=True) + NORM_EPS) * gain


def _const_spec(shape):
    zeros = (0,) * len(shape)
    return pl.BlockSpec(shape, lambda *_: zeros, pipeline_mode=pl.Buffered(1))


def _params(*semantics):
    return pltpu.CompilerParams(dimension_semantics=semantics, vmem_limit_bytes=VMEM_LIMIT)


def _sg_mixer_kernel(x_ref, g_ref, win_ref, vg_ref, wsp_ref, bt_ref, wout_ref, o_ref, y_ref):
    x = x_ref[...]
    hn = _rms(x, g_ref[...]).astype(BF16)
    z = jax.nn.gelu(jnp.dot(hn, win_ref[...], preferred_element_type=F32))
    u = z[:, :SG_WIDTH]
    v = _rms(z[:, SG_WIDTH:], vg_ref[...]).astype(BF16)
    row = lax.broadcasted_iota(I32, (SG_CHUNK, SG_CHUNK), 0)
    col = lax.broadcasted_iota(I32, (SG_CHUNK, SG_CHUNK), 1)
    causal = col <= row
    for g in range(SG_GROUPS):
        w_masked = jnp.where(causal, wsp_ref[g], 0.0).astype(BF16)
        bias = bt_ref[:, g:g + 1]
        cols = slice(g * SG_GROUP_WIDTH, (g + 1) * SG_GROUP_WIDTH)
        for c in range(TM_SG // SG_CHUNK):
            rows = slice(c * SG_CHUNK, (c + 1) * SG_CHUNK)
            mixed = jnp.dot(w_masked, v[rows, cols], preferred_element_type=F32) + bias
            y_ref[rows, cols] = (u[rows, cols] * mixed).astype(BF16)
    o_ref[...] = x + jnp.dot(y_ref[...], wout_ref[...], preferred_element_type=F32)


def _sg_mixer(x, gain, w_in, v_gain, w_spatial, b_spatial_t, w_out):
    n = x.shape[0]
    tok = lambda i: (i, 0)
    return pl.pallas_call(
        _sg_mixer_kernel,
        grid=(n // TM_SG,),
        in_specs=[
            pl.BlockSpec((TM_SG, D_MODEL), tok),
            _const_spec((1, D_MODEL)),
            _const_spec((D_MODEL, 2 * SG_WIDTH)),
            _const_spec((1, SG_WIDTH)),
            _const_spec((SG_GROUPS, SG_CHUNK, SG_CHUNK)),
            _const_spec((SG_CHUNK, SG_GROUPS)),
            _const_spec((SG_WIDTH, D_MODEL)),
        ],
        out_specs=pl.BlockSpec((TM_SG, D_MODEL), tok),
        out_shape=jax.ShapeDtypeStruct((n, D_MODEL), F32),
        scratch_shapes=[pltpu.VMEM((TM_SG, SG_WIDTH), BF16)],
        compiler_params=_params("parallel"),
        name="sg_mixer",
    )(x, gain, w_in, v_gain, w_spatial, b_spatial_t, w_out)


def _dense_ffn_kernel(x_ref, g_ref, wg_ref, wu_ref, wd_ref, o_ref):
    x = x_ref[...]
    hn = _rms(x, g_ref[...]).astype(BF16)
    gate = jnp.dot(hn, wg_ref[...], preferred_element_type=F32)
    up = jnp.dot(hn, wu_ref[...], preferred_element_type=F32)
    act = (jax.nn.silu(gate) * up).astype(BF16)
    o_ref[...] = x + jnp.dot(act, wd_ref[...], preferred_element_type=F32)


def _dense_ffn(x, gain, w_gate, w_up, w_down):
    n = x.shape[0]
    tok = lambda i: (i, 0)
    return pl.pallas_call(
        _dense_ffn_kernel,
        grid=(n // TM_FFN,),
        in_specs=[
            pl.BlockSpec((TM_FFN, D_MODEL), tok),
            _const_spec((1, D_MODEL)),
            _const_spec((D_MODEL, FFN_DENSE)),
            _const_spec((D_MODEL, FFN_DENSE)),
            _const_spec((FFN_DENSE, D_MODEL)),
        ],
        out_specs=pl.BlockSpec((TM_FFN, D_MODEL), tok),
        out_shape=jax.ShapeDtypeStruct((n, D_MODEL), F32),
        compiler_params=_params("parallel"),
        name="dense_ffn",
    )(x, gain, w_gate, w_up, w_down)


def _qkv_kernel(x_ref, pos_ref, g_ref, w_ref, freq_ref, qg_ref, kg_ref, seg_ref,
                q_ref, k_ref, vt_ref):
    hn = _rms(x_ref[...], g_ref[...]).astype(BF16)
    qkv = jnp.dot(hn, w_ref[...], preferred_element_type=F32)

    ang = pos_ref[...].astype(F32) * freq_ref[...]
    cos = jnp.cos(ang)
    sin = jnp.sin(ang)
    d = lax.broadcasted_iota(I32, (1, HEAD_WIDTH), 1) % DA_HEAD_DIM
    sin_lo = jnp.where(d < ROT_HALF, -sin, 0.0)
    sin_hi = jnp.where((d >= ROT_HALF) & (d < ROT_DIM), sin, 0.0)
    seg = seg_ref[...]

    def norm_rope(xh, gain):
        ssq = jnp.dot((xh * xh).astype(BF16), seg, preferred_element_type=F32)
        xn = xh * lax.rsqrt(ssq * (1.0 / DA_HEAD_DIM) + NORM_EPS) * gain
        nxt = pltpu.roll(xn, HEAD_WIDTH - ROT_HALF, 1)
        prv = pltpu.roll(xn, ROT_HALF, 1)
        return xn * cos + nxt * sin_lo + prv * sin_hi

    scale = 1.0 / math.sqrt(DA_HEAD_DIM)
    for h in range(DA_HEADS):
        cols = slice(h * HEAD_WIDTH, (h + 1) * HEAD_WIDTH)
        q = norm_rope(qkv[:, cols], qg_ref[...])
        q_ref[:, cols] = (q * scale).astype(BF16)
        kcols = slice(D_MODEL + h * HEAD_WIDTH, D_MODEL + (h + 1) * HEAD_WIDTH)
        k_ref[:, cols] = norm_rope(qkv[:, kcols], kg_ref[...]).astype(BF16)
        vcols = slice(2 * D_MODEL + h * HEAD_WIDTH, 2 * D_MODEL + (h + 1) * HEAD_WIDTH)
        vt_ref[0, h, 0] = qkv[:, vcols].T.astype(BF16)


def _qkv(x, pos, gain, w_qkv, freq_lane, q_gain, k_gain, seg_ones, batch, seq):
    n = x.shape[0]
    tiles = seq // TK
    tok = lambda i: (i, 0)
    out = jax.ShapeDtypeStruct((n, D_MODEL), BF16)
    return pl.pallas_call(
        _qkv_kernel,
        grid=(n // TK,),
        in_specs=[
            pl.BlockSpec((TK, D_MODEL), tok),
            pl.BlockSpec((TK, 1), tok),
            _const_spec((1, D_MODEL)),
            _const_spec((D_MODEL, 3 * D_MODEL)),
            _const_spec((1, HEAD_WIDTH)),
            _const_spec((1, HEAD_WIDTH)),
            _const_spec((1, HEAD_WIDTH)),
            _const_spec((HEAD_WIDTH, HEAD_WIDTH)),
        ],
        out_specs=[
            pl.BlockSpec((TK, D_MODEL), tok),
            pl.BlockSpec((TK, D_MODEL), tok),
            pl.BlockSpec((1, DA_HEADS, 1, HEAD_WIDTH, TK),
                         lambda i: (i // tiles, 0, i % tiles, 0, 0)),
        ],
        out_shape=[out, out,
                   jax.ShapeDtypeStruct((batch, DA_HEADS, tiles, HEAD_WIDTH, TK), BF16)],
        compiler_params=_params("parallel"),
        name="qkv_norm_rope",
    )(x, pos, gain, w_qkv, freq_lane, q_gain, k_gain, seg_ones)


def _attn_kernel(lam_ref, sub_ref, q_ref, k_ref, vt_ref, o_ref, qqt_ref, m_ref, l_ref, acc_ref):
    qi = pl.program_id(2)
    qt = q_ref[0].astype(F32).T
    first = lax.broadcasted_iota(I32, (HEAD_WIDTH, TQ), 0) < DA_HEAD_DIM
    qqt_ref[:, :TQ] = jnp.where(first, qt, 0.0).astype(BF16)
    qqt_ref[:, TQ:] = jnp.where(first, 0.0, qt).astype(BF16)
    m_ref[...] = jnp.full(m_ref.shape, -jnp.inf, F32)
    l_ref[...] = jnp.zeros(l_ref.shape, F32)
    acc_ref[...] = jnp.zeros(acc_ref.shape, F32)

    n_chunks = 2 * TQ // QC
    chunk_cols = [slice(c * QC, (c + 1) * QC) for c in range(n_chunks)]

    def scores(k, cols):
        return jnp.dot(k, qqt_ref[:, cols], preferred_element_type=F32)

    def softmax(s, cols, q_offset):
        if q_offset is not None:
            key = lax.broadcasted_iota(I32, s.shape, 0)
            qry = lax.broadcasted_iota(I32, s.shape, 1) + q_offset
            s = jnp.where(key <= qry, s, -jnp.inf)
        m_old = m_ref[:, cols]
        m_new = jnp.maximum(m_old, jnp.max(s, axis=0, keepdims=True))
        alpha = jnp.exp(m_old - m_new)
        p = jnp.exp(s - m_new)
        l_ref[:, cols] = alpha * l_ref[:, cols] + jnp.sum(p, axis=0, keepdims=True)
        m_ref[:, cols] = m_new
        return p.astype(BF16), alpha

    def accumulate(p, alpha, vt, cols):
        acc_ref[:, cols] = alpha * acc_ref[:, cols] + jnp.dot(
            vt, p, preferred_element_type=F32)

    def k_block(j):
        return k_ref[0, pl.ds(pl.multiple_of(j * TK, TK), TK), :]

    def full_block(j, carry):
        s_cur, p_prev, alpha_prev = carry
        k = k_block(j)
        vt = vt_ref[0, 0, j]
        for c in range(n_chunks):
            if c + 1 < n_chunks:
                s_next = scores(k, chunk_cols[c + 1])
            else:
                s_next = scores(k_block(j + 1), chunk_cols[0])
            p, alpha = softmax(s_cur, chunk_cols[c], None)
            vt_prev = vt_ref[0, 0, jnp.maximum(j - 1, 0)] if c == 0 else vt
            accumulate(p_prev, alpha_prev, vt_prev, chunk_cols[c - 1])
            s_cur, p_prev, alpha_prev = s_next, p, alpha
        return s_cur, p_prev, alpha_prev

    carry = (scores(k_block(0), chunk_cols[0]),
             jnp.zeros((TK, QC), BF16), jnp.ones((1, QC), F32))
    s_cur, p_prev, alpha_prev = lax.fori_loop(0, qi, full_block, carry)

    k_diag = k_block(qi)
    vt_diag = vt_ref[0, 0, qi]
    q_offsets = [(c * QC) % TQ for c in range(n_chunks)]
    nkeys = [q0 + QC for q0 in q_offsets]
    s_cur = s_cur[:nkeys[0]]
    vt_prev = vt_ref[0, 0, jnp.maximum(qi - 1, 0)]
    for c in range(n_chunks):
        if c + 1 < n_chunks:
            s_next = scores(k_diag[:nkeys[c + 1]], chunk_cols[c + 1])
        p, alpha = softmax(s_cur, chunk_cols[c], q_offsets[c])
        accumulate(p_prev, alpha_prev, vt_prev, chunk_cols[c - 1])
        s_cur, p_prev, alpha_prev, vt_prev = s_next, p, alpha, vt_diag[:, :nkeys[c]]
    accumulate(p_prev, alpha_prev, vt_prev, chunk_cols[n_chunks - 1])

    lam_terms = lam_ref[...]
    lam = (jnp.exp(jnp.sum(lam_terms[0:1] * lam_terms[1:2]))
           - jnp.exp(jnp.sum(lam_terms[2:3] * lam_terms[3:4])) + LAMBDA_INIT)
    ot = (acc_ref[:, :TQ] / l_ref[:, :TQ] - lam * (acc_ref[:, TQ:] / l_ref[:, TQ:]))
    o_ref[0] = (_rms(ot.T, sub_ref[...]) * (1.0 - LAMBDA_INIT)).astype(o_ref.dtype)


def _attention(q, k, vt, lam_terms, subln, batch, seq):
    q = q.reshape(batch, seq, D_MODEL)
    k = k.reshape(batch, seq, D_MODEL)
    q_spec = pl.BlockSpec((1, TQ, HEAD_WIDTH), lambda b, h, i: (b, i, h))
    o = pl.pallas_call(
        _attn_kernel,
        grid=(batch, DA_HEADS, seq // TQ),
        in_specs=[
            _const_spec((4, DA_HEAD_DIM)),
            _const_spec((1, HEAD_WIDTH)),
            q_spec,
            pl.BlockSpec((1, seq, HEAD_WIDTH), lambda b, h, i: (b, 0, h)),
            pl.BlockSpec((1, 1, seq // TK, HEAD_WIDTH, TK), lambda b, h, i: (b, h, 0, 0, 0)),
        ],
        out_specs=q_spec,
        out_shape=jax.ShapeDtypeStruct((batch, seq, D_MODEL), BF16),
        scratch_shapes=[
            pltpu.VMEM((HEAD_WIDTH, 2 * TQ), BF16),
            pltpu.VMEM((1, 2 * TQ), F32),
            pltpu.VMEM((1, 2 * TQ), F32),
            pltpu.VMEM((HEAD_WIDTH, 2 * TQ), F32),
        ],
        compiler_params=_params("parallel", "parallel", "arbitrary"),
        name="diff_attention",
    )(lam_terms, subln, q, k, vt)
    return o.reshape(batch * seq, D_MODEL)


def _proj_route_kernel(o_ref, h_ref, wo_ref, g_ref, wr_ref,
                       h2_ref, hn_ref, ri_ref, rw_ref, cnt_ref, run_ref):
    @pl.when(pl.program_id(0) == 0)
    def _():
        run_ref[...] = jnp.zeros(run_ref.shape, F32)

    h2 = h_ref[...] + jnp.dot(o_ref[...], wo_ref[...], preferred_element_type=F32)
    h2_ref[...] = h2
    hn = _rms(h2, g_ref[...])
    _store_rows(hn_ref, hn)
    logits = jnp.dot(hn, wr_ref[...], preferred_element_type=F32,
                     precision=lax.Precision.HIGHEST)
    lane = lax.broadcasted_iota(I32, logits.shape, 1)
    logits = jnp.where(lane < N_EXPERTS, logits, -jnp.inf)
    m1 = jnp.max(logits, axis=-1, keepdims=True)
    i1 = jnp.min(jnp.where(logits == m1, lane, LANES), axis=-1, keepdims=True)
    rest = jnp.where(lane == i1, -jnp.inf, logits)
    m2 = jnp.max(rest, axis=-1, keepdims=True)
    i2 = jnp.min(jnp.where(rest == m2, lane, LANES), axis=-1, keepdims=True)
    e2 = jnp.exp(m2 - m1)
    denom = 1.0 + e2
    rw_ref[...] = jnp.where(lane == 0, 1.0 / denom, jnp.where(lane == 1, e2 / denom, 0.0))

    sel1 = lane == i1
    sel2 = lane == i2
    sel = jnp.where(sel1 | sel2, 1.0, 0.0)
    t_row = lax.broadcasted_iota(I32, (TM_PROJ, TM_PROJ), 0)
    t_col = lax.broadcasted_iota(I32, (TM_PROJ, TM_PROJ), 1)
    earlier = jnp.where(t_col < t_row, 1.0, 0.0).astype(BF16)
    rank = run_ref[...] + jnp.dot(earlier, sel.astype(BF16), preferred_element_type=F32)
    rank1 = jnp.sum(jnp.where(sel1, rank, 0.0), axis=-1, keepdims=True).astype(I32)
    rank2 = jnp.sum(jnp.where(sel2, rank, 0.0), axis=-1, keepdims=True).astype(I32)
    ri_ref[...] = jnp.where(lane == 0, i1, jnp.where(lane == 1, i2,
                            jnp.where(lane == 2, rank1, jnp.where(lane == 3, rank2, 0))))
    run = run_ref[...] + jnp.sum(sel, axis=0, keepdims=True)
    run_ref[...] = run
    cnt_ref[...] = jnp.broadcast_to(run, cnt_ref.shape)


def _proj_route(o, h, w_out, gain, w_router):
    n = h.shape[0]
    tok = lambda i: (i, 0)
    return pl.pallas_call(
        _proj_route_kernel,
        grid=(n // TM_PROJ,),
        in_specs=[
            pl.BlockSpec((TM_PROJ, D_MODEL), tok),
            pl.BlockSpec((TM_PROJ, D_MODEL), tok),
            _const_spec((D_MODEL, D_MODEL)),
            _const_spec((1, D_MODEL)),
            _const_spec((D_MODEL, LANES)),
        ],
        out_specs=[
            pl.BlockSpec((TM_PROJ, D_MODEL), tok),
            pl.BlockSpec((TM_PROJ,) + ROW_TILE, lambda i: (i, 0, 0)),
            pl.BlockSpec((TM_PROJ, LANES), tok),
            pl.BlockSpec((TM_PROJ, LANES), tok),
            pl.BlockSpec((8, LANES), lambda i: (0, 0)),
        ],
        out_shape=[
            jax.ShapeDtypeStruct((n, D_MODEL), F32),
            jax.ShapeDtypeStruct((n,) + ROW_TILE, F32),
            jax.ShapeDtypeStruct((n, LANES), I32),
            jax.ShapeDtypeStruct((n, LANES), F32),
            jax.ShapeDtypeStruct((8, LANES), F32),
        ],
        scratch_shapes=[pltpu.VMEM((1, LANES), F32)],
        compiler_params=_params("arbitrary"),
        name="proj_route",
    )(o, h, w_out, gain, w_router)


def _row_copy(src_ref, src_row, dst_ref, dst_row, sem):
    return pltpu.make_async_copy(src_ref.at[pl.ds(src_row, 1)], dst_ref.at[pl.ds(dst_row, 1)], sem)


def _dispatch_kernel(d0_ref, d1_ref, zrow_ref, hn_ref, xs_ref, zero_ref, zsem, sem):
    i = pl.program_id(0)

    def zero_fill(z):
        return pltpu.make_async_copy(zero_ref, xs_ref.at[pl.ds(zrow_ref[z], TM_GROUP)], zsem)

    @pl.when(i == 0)
    def _():
        zero_ref[...] = jnp.zeros(zero_ref.shape, F32)
        for z in range(2 * N_EXPERTS):
            @pl.when(zrow_ref[z] >= 0)
            def _():
                zero_fill(z).start()
        for z in range(2 * N_EXPERTS):
            @pl.when(zrow_ref[z] >= 0)
            def _():
                zero_fill(z).wait()

    base = i * TM_MOVE

    def issue(t, carry):
        _row_copy(hn_ref, t, xs_ref, d0_ref[base + t], sem).start()
        _row_copy(hn_ref, t, xs_ref, d1_ref[base + t], sem).start()
        return carry

    lax.fori_loop(0, TM_MOVE, issue, 0, unroll=8)

    def drain(t, carry):
        _row_copy(hn_ref, t, xs_ref, d0_ref[base + t], sem).wait()
        _row_copy(hn_ref, t, xs_ref, d1_ref[base + t], sem).wait()
        return carry

    lax.fori_loop(0, TM_MOVE, drain, 0, unroll=8)


def _dispatch(d0, d1, zero_rows, hn, rows):
    n = hn.shape[0]
    return pl.pallas_call(
        _dispatch_kernel,
        grid_spec=pltpu.PrefetchScalarGridSpec(
            num_scalar_prefetch=3,
            grid=(n // TM_MOVE,),
            in_specs=[pl.BlockSpec((TM_MOVE,) + ROW_TILE, lambda i, d0, d1, z: (i, 0, 0))],
            out_specs=pl.BlockSpec(memory_space=pl.ANY),
            scratch_shapes=[
                pltpu.VMEM((TM_GROUP,) + ROW_TILE, F32),
                pltpu.SemaphoreType.DMA,
                pltpu.SemaphoreType.DMA,
            ],
        ),
        out_shape=jax.ShapeDtypeStruct((rows,) + ROW_TILE, F32),
        compiler_params=_params("arbitrary"),
        name="moe_dispatch",
    )(d0, d1, zero_rows, hn)


def _group_ffn_kernel(te_ref, tv_ref, x_ref, wg_ref, wu_ref, wd_ref, o_ref, xb_ref, acc_ref):
    g = pl.program_id(0)
    j = pl.program_id(1)

    @pl.when((tv_ref[g] == 0) & (j == 0))
    def _():
        o_ref[...] = jnp.zeros(o_ref.shape, F32)

    @pl.when(tv_ref[g] > 0)
    def _():
        @pl.when(j == 0)
        def _():
            for s in range(ROW_TILE[0]):
                xb_ref[:, s * LANES:(s + 1) * LANES] = x_ref[:, s, :].astype(BF16)

        x = xb_ref[...]
        gate = jnp.dot(x, wg_ref[0], preferred_element_type=F32)
        up = jnp.dot(x, wu_ref[0], preferred_element_type=F32)
        act = (jax.nn.silu(gate) * up).astype(BF16)
        y = jnp.dot(act, wd_ref[0], preferred_element_type=F32)

        @pl.when(j == 0)
        def _():
            acc_ref[...] = y

        @pl.when(j > 0)
        def _():
            acc_ref[...] += y

        @pl.when(j == pl.num_programs(1) - 1)
        def _():
            _store_rows(o_ref, acc_ref[...])


def _group_ffn(tile_expert, tile_valid, xs, w_gate, w_up, w_down):
    rows = xs.shape[0]
    n_chunks = FFN_EXPERT // FFN_EXPERT_CHUNK
    last = n_chunks - 1
    chunk = lambda g, j, tv: jnp.where(tv[g] > 0, j, last)
    row_map = lambda g, j, te, tv: (g, 0, 0)
    return pl.pallas_call(
        _group_ffn_kernel,
        grid_spec=pltpu.PrefetchScalarGridSpec(
            num_scalar_prefetch=2,
            grid=(rows // TM_GROUP, n_chunks),
            in_specs=[
                pl.BlockSpec((TM_GROUP,) + ROW_TILE, row_map),
                pl.BlockSpec((1, D_MODEL, FFN_EXPERT_CHUNK),
                             lambda g, j, te, tv: (te[g], 0, chunk(g, j, tv))),
                pl.BlockSpec((1, D_MODEL, FFN_EXPERT_CHUNK),
                             lambda g, j, te, tv: (te[g], 0, chunk(g, j, tv))),
                pl.BlockSpec((1, FFN_EXPERT_CHUNK, D_MODEL),
                             lambda g, j, te, tv: (te[g], chunk(g, j, tv), 0)),
            ],
            out_specs=pl.BlockSpec((TM_GROUP,) + ROW_TILE, row_map),
            scratch_shapes=[
                pltpu.VMEM((TM_GROUP, D_MODEL), BF16),
                pltpu.VMEM((TM_GROUP, D_MODEL), F32),
            ],
        ),
        out_shape=jax.ShapeDtypeStruct((rows,) + ROW_TILE, F32),
        compiler_params=_params("arbitrary", "arbitrary"),
        name="moe_group_ffn",
    )(tile_expert, tile_valid, xs, w_gate, w_up, w_down)


def _combine_kernel(d0_ref, d1_ref, h_ref, rw_ref, ys_ref, o_ref, r0_ref, r1_ref, sem):
    base = pl.program_id(0) * TM_MOVE

    def issue(t, carry):
        _row_copy(ys_ref, d0_ref[base + t], r0_ref, t, sem).start()
        _row_copy(ys_ref, d1_ref[base + t], r1_ref, t, sem).start()
        return carry

    lax.fori_loop(0, TM_MOVE, issue, 0, unroll=8)

    def drain(t, carry):
        _row_copy(ys_ref, d0_ref[base + t], r0_ref, t, sem).wait()
        _row_copy(ys_ref, d1_ref[base + t], r1_ref, t, sem).wait()
        return carry

    lax.fori_loop(0, TM_MOVE, drain, 0, unroll=8)
    rw = rw_ref[...]
    w1 = rw[:, 0:1]
    w2 = rw[:, 1:2]
    for s in range(ROW_TILE[0]):
        cols = slice(s * LANES, (s + 1) * LANES)
        o_ref[:, cols] = h_ref[:, cols] + w1 * r0_ref[:, s, :] + w2 * r1_ref[:, s, :]


def _combine(d0, d1, h, route_w, ys):
    n = h.shape[0]
    tok = lambda i, d0, d1: (i, 0)
    return pl.pallas_call(
        _combine_kernel,
        grid_spec=pltpu.PrefetchScalarGridSpec(
            num_scalar_prefetch=2,
            grid=(n // TM_MOVE,),
            in_specs=[
                pl.BlockSpec((TM_MOVE, D_MODEL), tok),
                pl.BlockSpec((TM_MOVE, LANES), tok),
                pl.BlockSpec(memory_space=pl.ANY),
            ],
            out_specs=pl.BlockSpec((TM_MOVE, D_MODEL), tok),
            scratch_shapes=[
                pltpu.VMEM((TM_MOVE,) + ROW_TILE, F32),
                pltpu.VMEM((TM_MOVE,) + ROW_TILE, F32),
                pltpu.SemaphoreType.DMA,
            ],
        ),
        out_shape=jax.ShapeDtypeStruct((n, D_MODEL), F32),
        compiler_params=_params("arbitrary"),
        name="moe_combine",
    )(d0, d1, h, route_w, ys)


def _routing_tables(route_i, counts, n):
    counts = counts.astype(I32)
    padded = (counts + TM_GROUP - 1) // TM_GROUP * TM_GROUP
    ends = jnp.cumsum(padded)
    starts = ends - padded
    d0 = starts[route_i[:, 0]] + route_i[:, 2]
    d1 = starts[route_i[:, 1]] + route_i[:, 3]
    n_tiles = TOP_K * n // TM_GROUP + N_EXPERTS
    used = ends[-1] // TM_GROUP
    g = jnp.arange(n_tiles, dtype=I32)
    first_row = jnp.minimum(g, used - 1) * TM_GROUP
    tile_expert = jnp.sum((ends[None, :] <= first_row[:, None]).astype(I32), axis=1)
    tile_valid = (g < used).astype(I32)
    tail = g[-N_EXPERTS:]
    zero_rows = jnp.concatenate([
        jnp.where(padded > 0, ends - TM_GROUP, -1),
        jnp.where(tail >= used, tail * TM_GROUP, -1)]).astype(I32)
    return d0, d1, zero_rows, tile_expert, tile_valid, n_tiles * TM_GROUP


def kernel(x, positions, l0_mix_norm, l0_sg_w_in, l0_sg_v_norm, l0_sg_w_spatial, l0_sg_b_spatial, l0_sg_w_out, l0_ffn_norm, l0_ffn_w_gate, l0_ffn_w_up, l0_ffn_w_down, l1_mix_norm, l1_da_w_qkv, l1_da_q_norm, l1_da_k_norm, l1_da_lambda_q1, l1_da_lambda_k1, l1_da_lambda_q2, l1_da_lambda_k2, l1_da_subln, l1_da_w_out, l1_moe_norm, l1_moe_w_router, l1_moe_w_gate, l1_moe_w_up, l1_moe_w_down):
    batch, seq, _ = x.shape
    n = batch * seq
    row = lambda a: a.reshape(1, -1)
    h = x.reshape(n, D_MODEL)

    h = _sg_mixer(h, row(l0_mix_norm), l0_sg_w_in.astype(BF16), row(l0_sg_v_norm),
                  l0_sg_w_spatial, l0_sg_b_spatial.T, l0_sg_w_out.astype(BF16))
    h = _dense_ffn(h, row(l0_ffn_norm), l0_ffn_w_gate.astype(BF16),
                   l0_ffn_w_up.astype(BF16), l0_ffn_w_down.astype(BF16))

    inv_freq = 1.0 / (ROPE_THETA ** (jnp.arange(0, ROT_DIM, 2, dtype=F32) / ROT_DIM))
    d = jnp.arange(HEAD_WIDTH) % DA_HEAD_DIM
    freq_lane = jnp.where(d < ROT_DIM, inv_freq[d % ROT_HALF], 0.0).reshape(1, HEAD_WIDTH)
    seg = jnp.arange(HEAD_WIDTH) // DA_HEAD_DIM
    seg_ones = (seg[:, None] == seg[None, :]).astype(BF16)
    q, k, vt = _qkv(h, positions.reshape(n, 1), row(l1_mix_norm), l1_da_w_qkv.astype(BF16),
                    freq_lane, row(jnp.tile(l1_da_q_norm, 2)), row(jnp.tile(l1_da_k_norm, 2)),
                    seg_ones, batch, seq)
    lam_terms = jnp.stack([l1_da_lambda_q1, l1_da_lambda_k1, l1_da_lambda_q2, l1_da_lambda_k2])
    o = _attention(q, k, vt, lam_terms, row(l1_da_subln), batch, seq)

    w_router = jnp.pad(l1_moe_w_router, ((0, 0), (0, LANES - N_EXPERTS)))
    h, hn, route_i, route_w, counts = _proj_route(
        o, h, l1_da_w_out.astype(BF16), row(l1_moe_norm), w_router)
    d0, d1, zero_rows, tile_expert, tile_valid, rows = _routing_tables(
        route_i, counts[0, :N_EXPERTS], n)
    xs = _dispatch(d0, d1, zero_rows, hn, rows)
    ys = _group_ffn(tile_expert, tile_valid, xs, l1_moe_w_gate.astype(BF16),
                    l1_moe_w_up.astype(BF16), l1_moe_w_down.astype(BF16))
    h = _combine(d0, d1, h, route_w, ys)
    return h.reshape(batch, seq, D_MODEL)
```

```python
import math

import jax
import jax.numpy as jnp
from jax import lax
from jax.experimental import pallas as pl
from jax.experimental.pallas import tpu as pltpu

F32 = jnp.float32
BF16 = jnp.bfloat16
I32 = jnp.int32

D_MODEL = 1024
SG_WIDTH = 2 * D_MODEL
SG_GROUPS = 8
SG_CHUNK = 128
SG_GROUP_WIDTH = SG_WIDTH // SG_GROUPS
DA_HEADS = 8
DA_HEAD_DIM = 64
HEAD_WIDTH = 2 * DA_HEAD_DIM
ROT_DIM = DA_HEAD_DIM // 4
ROT_HALF = ROT_DIM // 2
ROPE_THETA = 500000.0
FFN_DENSE = 2816
N_EXPERTS = 8
TOP_K = 2
FFN_EXPERT = 3584
NORM_EPS = 1e-6
LAMBDA_INIT = 0.8 - 0.6 * math.exp(-0.3 * 1)

LANES = 128
VMEM_LIMIT = 56 * 1024 * 1024

TM_SG = 512
TM_FFN = 512
TM_QKV = 512
TK = 1024
TQ = 1024
QC = 256
VT_ROWS = HEAD_WIDTH + 16
TM_PROJ = 512
TM_GROUP = 1024
FFN_EXPERT_CHUNK = 512
TM_MOVE = 512
SLAB = D_MODEL // LANES
assert SLAB == 8


def _slab_rows(t):
    return pl.ds(pl.multiple_of(t * SLAB, SLAB), SLAB)


def _load_slab_cols(ref, s, tokens):
    return ref[pl.ds(s, tokens, stride=SLAB), :]


def _store_slabs(ref, x):
    for s in range(SLAB):
        ref[pl.ds(s, x.shape[0], stride=SLAB), :] = x[:, s * LANES:(s + 1) * LANES]


def _rms(x, gain):
    return x * lax.rsqrt(jnp.mean(x * x, axis=-1, keepdims=True) + NORM_EPS) * gain


def _const_spec(shape):
    zeros = (0,) * len(shape)
    return pl.BlockSpec(shape, lambda *_: zeros, pipeline_mode=pl.Buffered(1))


def _params(*semantics):
    return pltpu.CompilerParams(dimension_semantics=semantics, vmem_limit_bytes=VMEM_LIMIT)


def _sg_mixer_kernel(x_ref, g_ref, win_ref, vg_ref, wsp_ref, bt_ref, wout_ref, o_ref, y_ref):
    x = x_ref[...]
    hn = _rms(x, g_ref[...]).astype(BF16)
    z = jax.nn.gelu(jnp.dot(hn, win_ref[...], preferred_element_type=F32))
    u = z[:, :SG_WIDTH]
    v = _rms(z[:, SG_WIDTH:], vg_ref[...]).astype(BF16)
    row = lax.broadcasted_iota(I32, (SG_CHUNK, SG_CHUNK), 0)
    col = lax.broadcasted_iota(I32, (SG_CHUNK, SG_CHUNK), 1)
    causal = col <= row
    for g in range(SG_GROUPS):
        w_masked = jnp.where(causal, wsp_ref[g], 0.0).astype(BF16)
        bias = bt_ref[:, g:g + 1]
        cols = slice(g * SG_GROUP_WIDTH, (g + 1) * SG_GROUP_WIDTH)
        for c in range(TM_SG // SG_CHUNK):
            rows = slice(c * SG_CHUNK, (c + 1) * SG_CHUNK)
            mixed = jnp.dot(w_masked, v[rows, cols], preferred_element_type=F32) + bias
            y_ref[rows, cols] = (u[rows, cols] * mixed).astype(BF16)
    o_ref[...] = x + jnp.dot(y_ref[...], wout_ref[...], preferred_element_type=F32)


def _sg_mixer(x, gain, w_in, v_gain, w_spatial, b_spatial_t, w_out):
    n = x.shape[0]
    tok = lambda i: (i, 0)
    return pl.pallas_call(
        _sg_mixer_kernel,
        grid=(n // TM_SG,),
        in_specs=[
            pl.BlockSpec((TM_SG, D_MODEL), tok),
            _const_spec((1, D_MODEL)),
            _const_spec((D_MODEL, 2 * SG_WIDTH)),
            _const_spec((1, SG_WIDTH)),
            _const_spec((SG_GROUPS, SG_CHUNK, SG_CHUNK)),
            _const_spec((SG_CHUNK, SG_GROUPS)),
            _const_spec((SG_WIDTH, D_MODEL)),
        ],
        out_specs=pl.BlockSpec((TM_SG, D_MODEL), tok),
        out_shape=jax.ShapeDtypeStruct((n, D_MODEL), F32),
        scratch_shapes=[pltpu.VMEM((TM_SG, SG_WIDTH), BF16)],
        compiler_params=_params("parallel"),
        name="sg_mixer",
    )(x, gain, w_in, v_gain, w_spatial, b_spatial_t, w_out)


def _dense_ffn_kernel(x_ref, g_ref, wg_ref, wu_ref, wd_ref, o_ref):
    x = x_ref[...]
    hn = _rms(x, g_ref[...]).astype(BF16)
    gate = jnp.dot(hn, wg_ref[...], preferred_element_type=F32)
    up = jnp.dot(hn, wu_ref[...], preferred_element_type=F32)
    act = (jax.nn.silu(gate) * up).astype(BF16)
    o_ref[...] = x + jnp.dot(act, wd_ref[...], preferred_element_type=F32)


def _dense_ffn(x, gain, w_gate, w_up, w_down):
    n = x.shape[0]
    tok = lambda i: (i, 0)
    return pl.pallas_call(
        _dense_ffn_kernel,
        grid=(n // TM_FFN,),
        in_specs=[
            pl.BlockSpec((TM_FFN, D_MODEL), tok),
            _const_spec((1, D_MODEL)),
            _const_spec((D_MODEL, FFN_DENSE)),
            _const_spec((D_MODEL, FFN_DENSE)),
            _const_spec((FFN_DENSE, D_MODEL)),
        ],
        out_specs=pl.BlockSpec((TM_FFN, D_MODEL), tok),
        out_shape=jax.ShapeDtypeStruct((n, D_MODEL), F32),
        compiler_params=_params("parallel"),
        name="dense_ffn",
    )(x, gain, w_gate, w_up, w_down)


def _qkv_kernel(x_ref, pos_ref, g_ref, w_ref, freq_ref, qg_ref, kg_ref, seg_ref,
                q_ref, k_ref, vt_ref):
    hn = _rms(x_ref[...], g_ref[...]).astype(BF16)
    qkv = jnp.dot(hn, w_ref[...], preferred_element_type=F32)

    ang = pos_ref[...].astype(F32) * freq_ref[...]
    cos = jnp.cos(ang)
    sin = jnp.sin(ang)
    d = lax.broadcasted_iota(I32, (1, HEAD_WIDTH), 1) % DA_HEAD_DIM
    sin_lo = jnp.where(d < ROT_HALF, -sin, 0.0)
    sin_hi = jnp.where((d >= ROT_HALF) & (d < ROT_DIM), sin, 0.0)
    seg = seg_ref[...]

    def norm_rope(xh, gain):
        ssq = jnp.dot((xh * xh).astype(BF16), seg, preferred_element_type=F32)
        xn = xh * lax.rsqrt(ssq * (1.0 / DA_HEAD_DIM) + NORM_EPS) * gain
        nxt = pltpu.roll(xn, HEAD_WIDTH - ROT_HALF, 1)
        prv = pltpu.roll(xn, ROT_HALF, 1)
        return xn * cos + nxt * sin_lo + prv * sin_hi

    scale = math.log2(math.e) / math.sqrt(DA_HEAD_DIM)
    for h in range(DA_HEADS):
        cols = slice(h * HEAD_WIDTH, (h + 1) * HEAD_WIDTH)
        q = norm_rope(qkv[:, cols], qg_ref[...])
        q_ref[:, cols] = (q * scale).astype(BF16)
        kcols = slice(D_MODEL + h * HEAD_WIDTH, D_MODEL + (h + 1) * HEAD_WIDTH)
        k_ref[:, cols] = norm_rope(qkv[:, kcols], kg_ref[...]).astype(BF16)
        vcols = slice(2 * D_MODEL + h * HEAD_WIDTH, 2 * D_MODEL + (h + 1) * HEAD_WIDTH)
        vt_ref[0, h, 0, :HEAD_WIDTH, :] = qkv[:, vcols].T.astype(BF16)
        vt_ref[0, h, 0, HEAD_WIDTH:, :] = jnp.ones((VT_ROWS - HEAD_WIDTH, TM_QKV), BF16)


def _qkv(x, pos, gain, w_qkv, freq_lane, q_gain, k_gain, seg_ones, batch, seq):
    n = x.shape[0]
    tiles = seq // TM_QKV
    per_key_tile = TK // TM_QKV
    tok = lambda i: (i, 0)
    out = jax.ShapeDtypeStruct((n, D_MODEL), BF16)
    return pl.pallas_call(
        _qkv_kernel,
        grid=(n // TM_QKV,),
        in_specs=[
            pl.BlockSpec((TM_QKV, D_MODEL), tok),
            pl.BlockSpec((TM_QKV, 1), tok),
            _const_spec((1, D_MODEL)),
            _const_spec((D_MODEL, 3 * D_MODEL)),
            _const_spec((1, HEAD_WIDTH)),
            _const_spec((1, HEAD_WIDTH)),
            _const_spec((1, HEAD_WIDTH)),
            _const_spec((HEAD_WIDTH, HEAD_WIDTH)),
        ],
        out_specs=[
            pl.BlockSpec((TM_QKV, D_MODEL), tok),
            pl.BlockSpec((TM_QKV, D_MODEL), tok),
            pl.BlockSpec((1, DA_HEADS, 1, VT_ROWS, TM_QKV),
                         lambda i: (i // tiles, 0, (i % tiles) // per_key_tile, 0,
                                    i % per_key_tile)),
        ],
        out_shape=[out, out,
                   jax.ShapeDtypeStruct((batch, DA_HEADS, seq // TK, VT_ROWS, TK), BF16)],
        compiler_params=_params("parallel"),
        name="qkv_norm_rope",
    )(x, pos, gain, w_qkv, freq_lane, q_gain, k_gain, seg_ones)


def _attn_kernel(lam_ref, sub_ref, q_ref, k_ref, vt_ref, o_ref, qqt_ref, m_ref, acc_ref):
    qi = pl.program_id(2)
    qt = q_ref[0].astype(F32).T
    first = lax.broadcasted_iota(I32, (HEAD_WIDTH, TQ), 0) < DA_HEAD_DIM
    qqt_ref[:, :TQ] = jnp.where(first, qt, 0.0).astype(BF16)
    qqt_ref[:, TQ:] = jnp.where(first, 0.0, qt).astype(BF16)
    m_ref[...] = jnp.full(m_ref.shape, -jnp.inf, F32)
    acc_ref[...] = jnp.zeros(acc_ref.shape, F32)

    n_chunks = 2 * TQ // QC
    chunk_cols = [slice(c * QC, (c + 1) * QC) for c in range(n_chunks)]

    def scores(k, cols):
        return jnp.dot(k, qqt_ref[:, cols], preferred_element_type=F32)

    def softmax(s, cols, q_offset):
        if q_offset is not None:
            key = lax.broadcasted_iota(I32, s.shape, 0)
            qry = lax.broadcasted_iota(I32, s.shape, 1) + q_offset
            s = jnp.where(key <= qry, s, -jnp.inf)
        m_old = m_ref[:, cols]
        m_new = jnp.maximum(m_old, jnp.max(s, axis=0, keepdims=True))
        alpha = jnp.exp2(m_old - m_new)
        p = jnp.exp2((s - m_new).astype(BF16))
        m_ref[:, cols] = m_new
        return p, alpha

    def accumulate(p, alpha, vt, cols):
        acc_ref[:, cols] = alpha * acc_ref[:, cols] + jnp.dot(
            vt, p, preferred_element_type=F32)

    def k_block(j):
        return k_ref[0, pl.ds(pl.multiple_of(j * TK, TK), TK), :]

    def full_block(j, carry):
        s_cur, p_prev, alpha_prev = carry
        k = k_block(j)
        vt = vt_ref[0, 0, j]
        for c in range(n_chunks):
            if c + 1 < n_chunks:
                s_next = scores(k, chunk_cols[c + 1])
            else:
                s_next = scores(k_block(j + 1), chunk_cols[0])
            p, alpha = softmax(s_cur, chunk_cols[c], None)
            vt_prev = vt_ref[0, 0, jnp.maximum(j - 1, 0)] if c == 0 else vt
            accumulate(p_prev, alpha_prev, vt_prev, chunk_cols[c - 1])
            s_cur, p_prev, alpha_prev = s_next, p, alpha
        return s_cur, p_prev, alpha_prev

    carry = (scores(k_block(0), chunk_cols[0]),
             jnp.zeros((TK, QC), BF16), jnp.ones((1, QC), F32))
    s_cur, p_prev, alpha_prev = lax.fori_loop(0, qi, full_block, carry)

    k_diag = k_block(qi)
    vt_diag = vt_ref[0, 0, qi]
    q_offsets = [(c * QC) % TQ for c in range(n_chunks)]
    nkeys = [q0 + QC for q0 in q_offsets]
    s_cur = s_cur[:nkeys[0]]
    vt_prev = vt_ref[0, 0, jnp.maximum(qi - 1, 0)]
    for c in range(n_chunks):
        if c + 1 < n_chunks:
            s_next = scores(k_diag[:nkeys[c + 1]], chunk_cols[c + 1])
        p, alpha = softmax(s_cur, chunk_cols[c], q_offsets[c])
        accumulate(p_prev, alpha_prev, vt_prev, chunk_cols[c - 1])
        s_cur, p_prev, alpha_prev, vt_prev = s_next, p, alpha, vt_diag[:, :nkeys[c]]
    accumulate(p_prev, alpha_prev, vt_prev, chunk_cols[n_chunks - 1])

    lam_terms = lam_ref[...]
    lam = (jnp.exp(jnp.sum(lam_terms[0:1] * lam_terms[1:2]))
           - jnp.exp(jnp.sum(lam_terms[2:3] * lam_terms[3:4])) + LAMBDA_INIT)
    pv = acc_ref[:HEAD_WIDTH, :]
    l = acc_ref[HEAD_WIDTH:HEAD_WIDTH + 1, :]
    ot = pv[:, :TQ] / l[:, :TQ] - lam * (pv[:, TQ:] / l[:, TQ:])
    o_ref[0] = (_rms(ot.T, sub_ref[...]) * (1.0 - LAMBDA_INIT)).astype(o_ref.dtype)


def _attention(q, k, vt, lam_terms, subln, batch, seq):
    q = q.reshape(batch, seq, D_MODEL)
    k = k.reshape(batch, seq, D_MODEL)
    q_spec = pl.BlockSpec((1, TQ, HEAD_WIDTH), lambda b, h, i: (b, i, h))
    o = pl.pallas_call(
        _attn_kernel,
        grid=(batch, DA_HEADS, seq // TQ),
        in_specs=[
            _const_spec((4, DA_HEAD_DIM)),
            _const_spec((1, HEAD_WIDTH)),
            q_spec,
            pl.BlockSpec((1, seq, HEAD_WIDTH), lambda b, h, i: (b, 0, h)),
            pl.BlockSpec((1, 1, seq // TK, VT_ROWS, TK), lambda b, h, i: (b, h, 0, 0, 0)),
        ],
        out_specs=q_spec,
        out_shape=jax.ShapeDtypeStruct((batch, seq, D_MODEL), BF16),
        scratch_shapes=[
            pltpu.VMEM((HEAD_WIDTH, 2 * TQ), BF16),
            pltpu.VMEM((1, 2 * TQ), F32),
            pltpu.VMEM((VT_ROWS, 2 * TQ), F32),
        ],
        compiler_params=_params("parallel", "parallel", "arbitrary"),
        name="diff_attention",
    )(lam_terms, subln, q, k, vt)
    return o.reshape(batch * seq, D_MODEL)


def _proj_route_kernel(o_ref, h_ref, wo_ref, g_ref, wr_ref,
                       h2_ref, hn_ref, ri_ref, rw_ref, cnt_ref, run_ref):
    @pl.when(pl.program_id(0) == 0)
    def _():
        run_ref[...] = jnp.zeros(run_ref.shape, F32)

    h2 = h_ref[...] + jnp.dot(o_ref[...], wo_ref[...], preferred_element_type=F32)
    h2_ref[...] = h2
    hn = _rms(h2, g_ref[...])
    _store_slabs(hn_ref, hn)
    logits = jnp.dot(hn, wr_ref[...], preferred_element_type=F32,
                     precision=lax.Precision.HIGHEST)
    lane = lax.broadcasted_iota(I32, logits.shape, 1)
    logits = jnp.where(lane < N_EXPERTS, logits, -jnp.inf)
    m1 = jnp.max(logits, axis=-1, keepdims=True)
    i1 = jnp.min(jnp.where(logits == m1, lane, LANES), axis=-1, keepdims=True)
    rest = jnp.where(lane == i1, -jnp.inf, logits)
    m2 = jnp.max(rest, axis=-1, keepdims=True)
    i2 = jnp.min(jnp.where(rest == m2, lane, LANES), axis=-1, keepdims=True)
    e2 = jnp.exp(m2 - m1)
    denom = 1.0 + e2
    rw_ref[...] = jnp.where(lane == 0, 1.0 / denom, jnp.where(lane == 1, e2 / denom, 0.0))

    sel1 = lane == i1
    sel2 = lane == i2
    sel = jnp.where(sel1 | sel2, 1.0, 0.0)
    t_row = lax.broadcasted_iota(I32, (TM_PROJ, TM_PROJ), 0)
    t_col = lax.broadcasted_iota(I32, (TM_PROJ, TM_PROJ), 1)
    earlier = jnp.where(t_col < t_row, 1.0, 0.0).astype(BF16)
    rank = run_ref[...] + jnp.dot(earlier, sel.astype(BF16), preferred_element_type=F32)
    rank1 = jnp.sum(jnp.where(sel1, rank, 0.0), axis=-1, keepdims=True).astype(I32)
    rank2 = jnp.sum(jnp.where(sel2, rank, 0.0), axis=-1, keepdims=True).astype(I32)
    ri_ref[...] = jnp.where(lane == 0, i1, jnp.where(lane == 1, i2,
                            jnp.where(lane == 2, rank1, jnp.where(lane == 3, rank2, 0))))
    run = run_ref[...] + jnp.sum(sel, axis=0, keepdims=True)
    run_ref[...] = run
    cnt_ref[...] = jnp.broadcast_to(run, cnt_ref.shape)


def _proj_route(o, h, w_out, gain, w_router):
    n = h.shape[0]
    tok = lambda i: (i, 0)
    return pl.pallas_call(
        _proj_route_kernel,
        grid=(n // TM_PROJ,),
        in_specs=[
            pl.BlockSpec((TM_PROJ, D_MODEL), tok),
            pl.BlockSpec((TM_PROJ, D_MODEL), tok),
            _const_spec((D_MODEL, D_MODEL)),
            _const_spec((1, D_MODEL)),
            _const_spec((D_MODEL, LANES)),
        ],
        out_specs=[
            pl.BlockSpec((TM_PROJ, D_MODEL), tok),
            pl.BlockSpec((TM_PROJ * SLAB, LANES), tok),
            pl.BlockSpec((TM_PROJ, LANES), tok),
            pl.BlockSpec((TM_PROJ, LANES), tok),
            pl.BlockSpec((8, LANES), lambda i: (0, 0)),
        ],
        out_shape=[
            jax.ShapeDtypeStruct((n, D_MODEL), F32),
            jax.ShapeDtypeStruct((n * SLAB, LANES), F32),
            jax.ShapeDtypeStruct((n, LANES), I32),
            jax.ShapeDtypeStruct((n, LANES), F32),
            jax.ShapeDtypeStruct((8, LANES), F32),
        ],
        scratch_shapes=[pltpu.VMEM((1, LANES), F32)],
        compiler_params=_params("arbitrary"),
        name="proj_route",
    )(o, h, w_out, gain, w_router)


def _row_copy(src_ref, src_row, dst_ref, dst_row, sem):
    return pltpu.make_async_copy(src_ref.at[_slab_rows(src_row)], dst_ref.at[_slab_rows(dst_row)], sem)


def _dispatch_kernel(d0_ref, d1_ref, zrow_ref, hn_ref, xs_ref, zero_ref, zsem, sem):
    i = pl.program_id(0)

    def zero_fill(z):
        rows = pl.ds(pl.multiple_of(zrow_ref[z] * SLAB, SLAB), TM_GROUP * SLAB)
        return pltpu.make_async_copy(zero_ref, xs_ref.at[rows], zsem)

    @pl.when(i == 0)
    def _():
        zero_ref[...] = jnp.zeros(zero_ref.shape, F32)
        for z in range(2 * N_EXPERTS):
            @pl.when(zrow_ref[z] >= 0)
            def _():
                zero_fill(z).start()
        for z in range(2 * N_EXPERTS):
            @pl.when(zrow_ref[z] >= 0)
            def _():
                zero_fill(z).wait()

    base = i * TM_MOVE

    def issue(t, carry):
        _row_copy(hn_ref, t, xs_ref, d0_ref[base + t], sem).start()
        _row_copy(hn_ref, t, xs_ref, d1_ref[base + t], sem).start()
        return carry

    lax.fori_loop(0, TM_MOVE, issue, 0, unroll=8)

    def drain(t, carry):
        _row_copy(hn_ref, t, xs_ref, d0_ref[base + t], sem).wait()
        _row_copy(hn_ref, t, xs_ref, d1_ref[base + t], sem).wait()
        return carry

    lax.fori_loop(0, TM_MOVE, drain, 0, unroll=8)


def _dispatch(d0, d1, zero_rows, hn, rows):
    n = hn.shape[0] // SLAB
    return pl.pallas_call(
        _dispatch_kernel,
        grid_spec=pltpu.PrefetchScalarGridSpec(
            num_scalar_prefetch=3,
            grid=(n // TM_MOVE,),
            in_specs=[pl.BlockSpec((TM_MOVE * SLAB, LANES), lambda i, d0, d1, z: (i, 0))],
            out_specs=pl.BlockSpec(memory_space=pl.ANY),
            scratch_shapes=[
                pltpu.VMEM((TM_GROUP * SLAB, LANES), F32),
                pltpu.SemaphoreType.DMA,
                pltpu.SemaphoreType.DMA,
            ],
        ),
        out_shape=jax.ShapeDtypeStruct((rows * SLAB, LANES), F32),
        compiler_params=_params("arbitrary"),
        name="moe_dispatch",
    )(d0, d1, zero_rows, hn)


def _group_ffn_kernel(te_ref, tv_ref, x_ref, wg_ref, wu_ref, wd_ref, o_ref, xb_ref, acc_ref):
    g = pl.program_id(0)
    j = pl.program_id(1)

    @pl.when((tv_ref[g] == 0) & (j == 0))
    def _():
        o_ref[...] = jnp.zeros(o_ref.shape, F32)

    @pl.when(tv_ref[g] > 0)
    def _():
        @pl.when(j == 0)
        def _():
            for s in range(SLAB):
                xb_ref[:, s * LANES:(s + 1) * LANES] = _load_slab_cols(
                    x_ref, s, TM_GROUP).astype(BF16)

        x = xb_ref[...]
        gate = jnp.dot(x, wg_ref[0], preferred_element_type=F32)
        up = jnp.dot(x, wu_ref[0], preferred_element_type=F32)
        act = (jax.nn.silu(gate) * up).astype(BF16)
        y = jnp.dot(act, wd_ref[0], preferred_element_type=F32)

        @pl.when(j == 0)
        def _():
            acc_ref[...] = y

        @pl.when(j > 0)
        def _():
            acc_ref[...] += y

        @pl.when(j == pl.num_programs(1) - 1)
        def _():
            _store_slabs(o_ref, acc_ref[...])


def _group_ffn(tile_expert, tile_valid, xs, w_gate, w_up, w_down):
    rows = xs.shape[0] // SLAB
    n_chunks = FFN_EXPERT // FFN_EXPERT_CHUNK
    last = n_chunks - 1
    chunk = lambda g, j, tv: jnp.where(tv[g] > 0, j, last)
    row_map = lambda g, j, te, tv: (g, 0)
    return pl.pallas_call(
        _group_ffn_kernel,
        grid_spec=pltpu.PrefetchScalarGridSpec(
            num_scalar_prefetch=2,
            grid=(rows // TM_GROUP, n_chunks),
            in_specs=[
                pl.BlockSpec((TM_GROUP * SLAB, LANES), row_map),
                pl.BlockSpec((1, D_MODEL, FFN_EXPERT_CHUNK),
                             lambda g, j, te, tv: (te[g], 0, chunk(g, j, tv))),
                pl.BlockSpec((1, D_MODEL, FFN_EXPERT_CHUNK),
                             lambda g, j, te, tv: (te[g], 0, chunk(g, j, tv))),
                pl.BlockSpec((1, FFN_EXPERT_CHUNK, D_MODEL),
                             lambda g, j, te, tv: (te[g], chunk(g, j, tv), 0)),
            ],
            out_specs=pl.BlockSpec((TM_GROUP * SLAB, LANES), row_map),
            scratch_shapes=[
                pltpu.VMEM((TM_GROUP, D_MODEL), BF16),
                pltpu.VMEM((TM_GROUP, D_MODEL), F32),
            ],
        ),
        out_shape=jax.ShapeDtypeStruct((rows * SLAB, LANES), F32),
        compiler_params=_params("arbitrary", "arbitrary"),
        name="moe_group_ffn",
    )(tile_expert, tile_valid, xs, w_gate, w_up, w_down)


def _combine_kernel(d0_ref, d1_ref, h_ref, rw_ref, ys_ref, o_ref, r0_ref, r1_ref, sem):
    base = pl.program_id(0) * TM_MOVE

    def issue(t, carry):
        _row_copy(ys_ref, d0_ref[base + t], r0_ref, t, sem).start()
        _row_copy(ys_ref, d1_ref[base + t], r1_ref, t, sem).start()
        return carry

    lax.fori_loop(0, TM_MOVE, issue, 0, unroll=8)

    def drain(t, carry):
        _row_copy(ys_ref, d0_ref[base + t], r0_ref, t, sem).wait()
        _row_copy(ys_ref, d1_ref[base + t], r1_ref, t, sem).wait()
        return carry

    lax.fori_loop(0, TM_MOVE, drain, 0, unroll=8)
    rw = rw_ref[...]
    w1 = rw[:, 0:1]
    w2 = rw[:, 1:2]
    for s in range(SLAB):
        cols = slice(s * LANES, (s + 1) * LANES)
        o_ref[:, cols] = (h_ref[:, cols] + w1 * _load_slab_cols(r0_ref, s, TM_MOVE)
                          + w2 * _load_slab_cols(r1_ref, s, TM_MOVE))


def _combine(d0, d1, h, route_w, ys):
    n = h.shape[0]
    tok = lambda i, d0, d1: (i, 0)
    return pl.pallas_call(
        _combine_kernel,
        grid_spec=pltpu.PrefetchScalarGridSpec(
            num_scalar_prefetch=2,
            grid=(n // TM_MOVE,),
            in_specs=[
                pl.BlockSpec((TM_MOVE, D_MODEL), tok),
                pl.BlockSpec((TM_MOVE, LANES), tok),
                pl.BlockSpec(memory_space=pl.ANY),
            ],
            out_specs=pl.BlockSpec((TM_MOVE, D_MODEL), tok),
            scratch_shapes=[
                pltpu.VMEM((TM_MOVE * SLAB, LANES), F32),
                pltpu.VMEM((TM_MOVE * SLAB, LANES), F32),
                pltpu.SemaphoreType.DMA,
            ],
        ),
        out_shape=jax.ShapeDtypeStruct((n, D_MODEL), F32),
        compiler_params=_params("arbitrary"),
        name="moe_combine",
    )(d0, d1, h, route_w, ys)


def _routing_tables(route_i, counts, n):
    counts = counts.astype(I32)
    padded = (counts + TM_GROUP - 1) // TM_GROUP * TM_GROUP
    ends = jnp.cumsum(padded)
    starts = ends - padded
    d0 = starts[route_i[:, 0]] + route_i[:, 2]
    d1 = starts[route_i[:, 1]] + route_i[:, 3]
    n_tiles = TOP_K * n // TM_GROUP + N_EXPERTS
    used = ends[-1] // TM_GROUP
    g = jnp.arange(n_tiles, dtype=I32)
    first_row = jnp.minimum(g, used - 1) * TM_GROUP
    tile_expert = jnp.sum((ends[None, :] <= first_row[:, None]).astype(I32), axis=1)
    tile_valid = (g < used).astype(I32)
    tail = g[-N_EXPERTS:]
    zero_rows = jnp.concatenate([
        jnp.where(padded > 0, ends - TM_GROUP, -1),
        jnp.where(tail >= used, tail * TM_GROUP, -1)]).astype(I32)
    return d0, d1, zero_rows, tile_expert, tile_valid, n_tiles * TM_GROUP


def kernel(x, positions, l0_mix_norm, l0_sg_w_in, l0_sg_v_norm, l0_sg_w_spatial, l0_sg_b_spatial, l0_sg_w_out, l0_ffn_norm, l0_ffn_w_gate, l0_ffn_w_up, l0_ffn_w_down, l1_mix_norm, l1_da_w_qkv, l1_da_q_norm, l1_da_k_norm, l1_da_lambda_q1, l1_da_lambda_k1, l1_da_lambda_q2, l1_da_lambda_k2, l1_da_subln, l1_da_w_out, l1_moe_norm, l1_moe_w_router, l1_moe_w_gate, l1_moe_w_up, l1_moe_w_down):
    batch, seq, _ = x.shape
    n = batch * seq
    row = lambda a: a.reshape(1, -1)
    h = x.reshape(n, D_MODEL)

    h = _sg_mixer(h, row(l0_mix_norm), l0_sg_w_in.astype(BF16), row(l0_sg_v_norm),
                  l0_sg_w_spatial, l0_sg_b_spatial.T, l0_sg_w_out.astype(BF16))
    h = _dense_ffn(h, row(l0_ffn_norm), l0_ffn_w_gate.astype(BF16),
                   l0_ffn_w_up.astype(BF16), l0_ffn_w_down.astype(BF16))

    inv_freq = 1.0 / (ROPE_THETA ** (jnp.arange(0, ROT_DIM, 2, dtype=F32) / ROT_DIM))
    d = jnp.arange(HEAD_WIDTH) % DA_HEAD_DIM
    freq_lane = jnp.where(d < ROT_DIM, inv_freq[d % ROT_HALF], 0.0).reshape(1, HEAD_WIDTH)
    seg = jnp.arange(HEAD_WIDTH) // DA_HEAD_DIM
    seg_ones = (seg[:, None] == seg[None, :]).astype(BF16)
    q, k, vt = _qkv(h, positions.reshape(n, 1), row(l1_mix_norm), l1_da_w_qkv.astype(BF16),
                    freq_lane, row(jnp.tile(l1_da_q_norm, 2)), row(jnp.tile(l1_da_k_norm, 2)),
                    seg_ones, batch, seq)
    lam_terms = jnp.stack([l1_da_lambda_q1, l1_da_lambda_k1, l1_da_lambda_q2, l1_da_lambda_k2])
    o = _attention(q, k, vt, lam_terms, row(l1_da_subln), batch, seq)

    w_router = jnp.pad(l1_moe_w_router, ((0, 0), (0, LANES - N_EXPERTS)))
    h, hn, route_i, route_w, counts = _proj_route(
        o, h, l1_da_w_out.astype(BF16), row(l1_moe_norm), w_router)
    d0, d1, zero_rows, tile_expert, tile_valid, rows = _routing_tables(
        route_i, counts[0, :N_EXPERTS], n)
    xs = _dispatch(d0, d1, zero_rows, hn, rows)
    ys = _group_ffn(tile_expert, tile_valid, xs, l1_moe_w_gate.astype(BF16),
                    l1_moe_w_up.astype(BF16), l1_moe_w_down.astype(BF16))
    h = _combine(d0, d1, h, route_w, ys)
    return h.reshape(batch, seq, D_MODEL)
```

```python
import math

import jax
import jax.numpy as jnp
from jax import lax
from jax.experimental import pallas as pl
from jax.experimental.pallas import tpu as pltpu

F32 = jnp.float32
BF16 = jnp.bfloat16
I32 = jnp.int32

D_MODEL = 1024
SG_WIDTH = 2 * D_MODEL
SG_GROUPS = 8
SG_CHUNK = 128
SG_GROUP_WIDTH = SG_WIDTH // SG_GROUPS
DA_HEADS = 8
DA_HEAD_DIM = 64
HEAD_WIDTH = 2 * DA_HEAD_DIM
ROT_DIM = DA_HEAD_DIM // 4
ROT_HALF = ROT_DIM // 2
ROPE_THETA = 500000.0
FFN_DENSE = 2816
N_EXPERTS = 8
TOP_K = 2
FFN_EXPERT = 3584
NORM_EPS = 1e-6
LAMBDA_INIT = 0.8 - 0.6 * math.exp(-0.3 * 1)

LANES = 128
VMEM_LIMIT = 56 * 1024 * 1024

TM_SG = 512
TM_FFN = 512
TM_QKV = 512
TK = 1024
TQ = 1024
QC = 256
VT_ROWS = HEAD_WIDTH + 16
TM_PROJ = 512
TM_GROUP = 512
FFN_EXPERT_CHUNK = 1792
TM_MOVE = 512
SLAB = D_MODEL // LANES
assert SLAB == 8


def _slab_rows(t):
    return pl.ds(pl.multiple_of(t * SLAB, SLAB), SLAB)


def _load_slab_cols(ref, s, tokens):
    return ref[pl.ds(s, tokens, stride=SLAB), :]


def _store_slabs(ref, x):
    for s in range(SLAB):
        ref[pl.ds(s, x.shape[0], stride=SLAB), :] = x[:, s * LANES:(s + 1) * LANES]


def _rms(x, gain):
    return x * lax.rsqrt(jnp.mean(x * x, axis=-1, keepdims=True) + NORM_EPS) * gain


def _const_spec(shape):
    zeros = (0,) * len(shape)
    return pl.BlockSpec(shape, lambda *_: zeros, pipeline_mode=pl.Buffered(1))


def _params(*semantics):
    return pltpu.CompilerParams(dimension_semantics=semantics, vmem_limit_bytes=VMEM_LIMIT)


def _sg_mixer_kernel(x_ref, g_ref, win_ref, vg_ref, wsp_ref, bt_ref, wout_ref, o_ref, y_ref):
    x = x_ref[...]
    hn = _rms(x, g_ref[...]).astype(BF16)
    z = jax.nn.gelu(jnp.dot(hn, win_ref[...], preferred_element_type=F32))
    u = z[:, :SG_WIDTH]
    v = _rms(z[:, SG_WIDTH:], vg_ref[...]).astype(BF16)
    row = lax.broadcasted_iota(I32, (SG_CHUNK, SG_CHUNK), 0)
    col = lax.broadcasted_iota(I32, (SG_CHUNK, SG_CHUNK), 1)
    causal = col <= row
    for g in range(SG_GROUPS):
        w_masked = jnp.where(causal, wsp_ref[g], 0.0).astype(BF16)
        bias = bt_ref[:, g:g + 1]
        cols = slice(g * SG_GROUP_WIDTH, (g + 1) * SG_GROUP_WIDTH)
        for c in range(TM_SG // SG_CHUNK):
            rows = slice(c * SG_CHUNK, (c + 1) * SG_CHUNK)
            mixed = jnp.dot(w_masked, v[rows, cols], preferred_element_type=F32) + bias
            y_ref[rows, cols] = (u[rows, cols] * mixed).astype(BF16)
    o_ref[...] = x + jnp.dot(y_ref[...], wout_ref[...], preferred_element_type=F32)


def _sg_mixer(x, gain, w_in, v_gain, w_spatial, b_spatial_t, w_out):
    n = x.shape[0]
    tok = lambda i: (i, 0)
    return pl.pallas_call(
        _sg_mixer_kernel,
        grid=(n // TM_SG,),
        in_specs=[
            pl.BlockSpec((TM_SG, D_MODEL), tok),
            _const_spec((1, D_MODEL)),
            _const_spec((D_MODEL, 2 * SG_WIDTH)),
            _const_spec((1, SG_WIDTH)),
            _const_spec((SG_GROUPS, SG_CHUNK, SG_CHUNK)),
            _const_spec((SG_CHUNK, SG_GROUPS)),
            _const_spec((SG_WIDTH, D_MODEL)),
        ],
        out_specs=pl.BlockSpec((TM_SG, D_MODEL), tok),
        out_shape=jax.ShapeDtypeStruct((n, D_MODEL), F32),
        scratch_shapes=[pltpu.VMEM((TM_SG, SG_WIDTH), BF16)],
        compiler_params=_params("parallel"),
        name="sg_mixer",
    )(x, gain, w_in, v_gain, w_spatial, b_spatial_t, w_out)


def _dense_ffn_kernel(x_ref, g_ref, wg_ref, wu_ref, wd_ref, o_ref):
    x = x_ref[...]
    hn = _rms(x, g_ref[...]).astype(BF16)
    gate = jnp.dot(hn, wg_ref[...], preferred_element_type=F32)
    up = jnp.dot(hn, wu_ref[...], preferred_element_type=F32)
    act = (jax.nn.silu(gate) * up).astype(BF16)
    o_ref[...] = x + jnp.dot(act, wd_ref[...], preferred_element_type=F32)


def _dense_ffn(x, gain, w_gate, w_up, w_down):
    n = x.shape[0]
    tok = lambda i: (i, 0)
    return pl.pallas_call(
        _dense_ffn_kernel,
        grid=(n // TM_FFN,),
        in_specs=[
            pl.BlockSpec((TM_FFN, D_MODEL), tok),
            _const_spec((1, D_MODEL)),
            _const_spec((D_MODEL, FFN_DENSE)),
            _const_spec((D_MODEL, FFN_DENSE)),
            _const_spec((FFN_DENSE, D_MODEL)),
        ],
        out_specs=pl.BlockSpec((TM_FFN, D_MODEL), tok),
        out_shape=jax.ShapeDtypeStruct((n, D_MODEL), F32),
        compiler_params=_params("parallel"),
        name="dense_ffn",
    )(x, gain, w_gate, w_up, w_down)


def _qkv_kernel(x_ref, pos_ref, g_ref, w_ref, freq_ref, qg_ref, kg_ref, seg_ref,
                q_ref, k_ref, vt_ref):
    hn = _rms(x_ref[...], g_ref[...]).astype(BF16)
    qkv = jnp.dot(hn, w_ref[...], preferred_element_type=F32)

    ang = pos_ref[...].astype(F32) * freq_ref[...]
    cos = jnp.cos(ang)
    sin = jnp.sin(ang)
    d = lax.broadcasted_iota(I32, (1, HEAD_WIDTH), 1) % DA_HEAD_DIM
    sin_lo = jnp.where(d < ROT_HALF, -sin, 0.0)
    sin_hi = jnp.where((d >= ROT_HALF) & (d < ROT_DIM), sin, 0.0)
    seg = seg_ref[...]

    def norm_rope(xh, gain):
        ssq = jnp.dot((xh * xh).astype(BF16), seg, preferred_element_type=F32)
        xn = xh * lax.rsqrt(ssq * (1.0 / DA_HEAD_DIM) + NORM_EPS) * gain
        nxt = pltpu.roll(xn, HEAD_WIDTH - ROT_HALF, 1)
        prv = pltpu.roll(xn, ROT_HALF, 1)
        return xn * cos + nxt * sin_lo + prv * sin_hi

    scale = math.log2(math.e) / math.sqrt(DA_HEAD_DIM)
    for h in range(DA_HEADS):
        cols = slice(h * HEAD_WIDTH, (h + 1) * HEAD_WIDTH)
        q = norm_rope(qkv[:, cols], qg_ref[...])
        q_ref[:, cols] = (q * scale).astype(BF16)
        kcols = slice(D_MODEL + h * HEAD_WIDTH, D_MODEL + (h + 1) * HEAD_WIDTH)
        k_ref[:, cols] = norm_rope(qkv[:, kcols], kg_ref[...]).astype(BF16)
        vcols = slice(2 * D_MODEL + h * HEAD_WIDTH, 2 * D_MODEL + (h + 1) * HEAD_WIDTH)
        vt_ref[0, h, 0, :HEAD_WIDTH, :] = qkv[:, vcols].T.astype(BF16)
        vt_ref[0, h, 0, HEAD_WIDTH:, :] = jnp.ones((VT_ROWS - HEAD_WIDTH, TM_QKV), BF16)


def _qkv(x, pos, gain, w_qkv, freq_lane, q_gain, k_gain, seg_ones, batch, seq):
    n = x.shape[0]
    tiles = seq // TM_QKV
    per_key_tile = TK // TM_QKV
    tok = lambda i: (i, 0)
    out = jax.ShapeDtypeStruct((n, D_MODEL), BF16)
    return pl.pallas_call(
        _qkv_kernel,
        grid=(n // TM_QKV,),
        in_specs=[
            pl.BlockSpec((TM_QKV, D_MODEL), tok),
            pl.BlockSpec((TM_QKV, 1), tok),
            _const_spec((1, D_MODEL)),
            _const_spec((D_MODEL, 3 * D_MODEL)),
            _const_spec((1, HEAD_WIDTH)),
            _const_spec((1, HEAD_WIDTH)),
            _const_spec((1, HEAD_WIDTH)),
            _const_spec((HEAD_WIDTH, HEAD_WIDTH)),
        ],
        out_specs=[
            pl.BlockSpec((TM_QKV, D_MODEL), tok),
            pl.BlockSpec((TM_QKV, D_MODEL), tok),
            pl.BlockSpec((1, DA_HEADS, 1, VT_ROWS, TM_QKV),
                         lambda i: (i // tiles, 0, (i % tiles) // per_key_tile, 0,
                                    i % per_key_tile)),
        ],
        out_shape=[out, out,
                   jax.ShapeDtypeStruct((batch, DA_HEADS, seq // TK, VT_ROWS, TK), BF16)],
        compiler_params=_params("parallel"),
        name="qkv_norm_rope",
    )(x, pos, gain, w_qkv, freq_lane, q_gain, k_gain, seg_ones)


def _attn_kernel(lam_ref, sub_ref, q_ref, k_ref, vt_ref, o_ref, qqt_ref, m_ref, acc_ref):
    qi = pl.program_id(2)
    qt = q_ref[0].astype(F32).T
    first = lax.broadcasted_iota(I32, (HEAD_WIDTH, TQ), 0) < DA_HEAD_DIM
    qqt_ref[:, :TQ] = jnp.where(first, qt, 0.0).astype(BF16)
    qqt_ref[:, TQ:] = jnp.where(first, 0.0, qt).astype(BF16)
    m_ref[...] = jnp.full(m_ref.shape, -jnp.inf, F32)
    acc_ref[...] = jnp.zeros(acc_ref.shape, F32)

    n_chunks = 2 * TQ // QC
    chunk_cols = [slice(c * QC, (c + 1) * QC) for c in range(n_chunks)]

    def scores(k, cols):
        return jnp.dot(k, qqt_ref[:, cols], preferred_element_type=F32)

    def softmax(s, cols, q_offset):
        if q_offset is not None:
            key = lax.broadcasted_iota(I32, s.shape, 0)
            qry = lax.broadcasted_iota(I32, s.shape, 1) + q_offset
            s = jnp.where(key <= qry, s, -jnp.inf)
        m_old = m_ref[:, cols]
        m_new = jnp.maximum(m_old, jnp.max(s, axis=0, keepdims=True))
        alpha = jnp.exp2(m_old - m_new)
        p = jnp.exp2((s - m_new).astype(BF16))
        m_ref[:, cols] = m_new
        return p, alpha

    def accumulate(p, alpha, vt, cols):
        acc_ref[:, cols] = alpha * acc_ref[:, cols] + jnp.dot(
            vt, p, preferred_element_type=F32)

    def k_block(j):
        return k_ref[0, pl.ds(pl.multiple_of(j * TK, TK), TK), :]

    def full_block(j, carry):
        s_cur, p_prev, alpha_prev = carry
        k = k_block(j)
        vt = vt_ref[0, 0, j]
        for c in range(n_chunks):
            if c + 1 < n_chunks:
                s_next = scores(k, chunk_cols[c + 1])
            else:
                s_next = scores(k_block(j + 1), chunk_cols[0])
            p, alpha = softmax(s_cur, chunk_cols[c], None)
            vt_prev = vt_ref[0, 0, jnp.maximum(j - 1, 0)] if c == 0 else vt
            accumulate(p_prev, alpha_prev, vt_prev, chunk_cols[c - 1])
            s_cur, p_prev, alpha_prev = s_next, p, alpha
        return s_cur, p_prev, alpha_prev

    carry = (scores(k_block(0), chunk_cols[0]),
             jnp.zeros((TK, QC), BF16), jnp.ones((1, QC), F32))
    s_cur, p_prev, alpha_prev = lax.fori_loop(0, qi, full_block, carry)

    k_diag = k_block(qi)
    vt_diag = vt_ref[0, 0, qi]
    q_offsets = [(c * QC) % TQ for c in range(n_chunks)]
    nkeys = [q0 + QC for q0 in q_offsets]
    s_cur = s_cur[:nkeys[0]]
    vt_prev = vt_ref[0, 0, jnp.maximum(qi - 1, 0)]
    for c in range(n_chunks):
        if c + 1 < n_chunks:
            s_next = scores(k_diag[:nkeys[c + 1]], chunk_cols[c + 1])
        p, alpha = softmax(s_cur, chunk_cols[c], q_offsets[c])
        accumulate(p_prev, alpha_prev, vt_prev, chunk_cols[c - 1])
        s_cur, p_prev, alpha_prev, vt_prev = s_next, p, alpha, vt_diag[:, :nkeys[c]]
    accumulate(p_prev, alpha_prev, vt_prev, chunk_cols[n_chunks - 1])

    lam_terms = lam_ref[...]
    lam = (jnp.exp(jnp.sum(lam_terms[0:1] * lam_terms[1:2]))
           - jnp.exp(jnp.sum(lam_terms[2:3] * lam_terms[3:4])) + LAMBDA_INIT)
    pv = acc_ref[:HEAD_WIDTH, :]
    l = acc_ref[HEAD_WIDTH:HEAD_WIDTH + 1, :]
    ot = pv[:, :TQ] / l[:, :TQ] - lam * (pv[:, TQ:] / l[:, TQ:])
    o_ref[0] = (_rms(ot.T, sub_ref[...]) * (1.0 - LAMBDA_INIT)).astype(o_ref.dtype)


def _attention(q, k, vt, lam_terms, subln, batch, seq):
    q = q.reshape(batch, seq, D_MODEL)
    k = k.reshape(batch, seq, D_MODEL)
    q_spec = pl.BlockSpec((1, TQ, HEAD_WIDTH), lambda b, h, i: (b, i, h))
    o = pl.pallas_call(
        _attn_kernel,
        grid=(batch, DA_HEADS, seq // TQ),
        in_specs=[
            _const_spec((4, DA_HEAD_DIM)),
            _const_spec((1, HEAD_WIDTH)),
            q_spec,
            pl.BlockSpec((1, seq, HEAD_WIDTH), lambda b, h, i: (b, 0, h)),
            pl.BlockSpec((1, 1, seq // TK, VT_ROWS, TK), lambda b, h, i: (b, h, 0, 0, 0)),
        ],
        out_specs=q_spec,
        out_shape=jax.ShapeDtypeStruct((batch, seq, D_MODEL), BF16),
        scratch_shapes=[
            pltpu.VMEM((HEAD_WIDTH, 2 * TQ), BF16),
            pltpu.VMEM((1, 2 * TQ), F32),
            pltpu.VMEM((VT_ROWS, 2 * TQ), F32),
        ],
        compiler_params=_params("parallel", "parallel", "arbitrary"),
        name="diff_attention",
    )(lam_terms, subln, q, k, vt)
    return o.reshape(batch * seq, D_MODEL)


def _proj_route_kernel(o_ref, h_ref, wo_ref, g_ref, wr_ref,
                       h2_ref, hn_ref, ri_ref, rw_ref, cnt_ref, run_ref):
    @pl.when(pl.program_id(0) == 0)
    def _():
        run_ref[...] = jnp.zeros(run_ref.shape, F32)

    h2 = h_ref[...] + jnp.dot(o_ref[...], wo_ref[...], preferred_element_type=F32)
    h2_ref[...] = h2
    hn = _rms(h2, g_ref[...])
    _store_slabs(hn_ref, hn)
    logits = jnp.dot(hn, wr_ref[...], preferred_element_type=F32,
                     precision=lax.Precision.HIGHEST)
    lane = lax.broadcasted_iota(I32, logits.shape, 1)
    logits = jnp.where(lane < N_EXPERTS, logits, -jnp.inf)
    m1 = jnp.max(logits, axis=-1, keepdims=True)
    i1 = jnp.min(jnp.where(logits == m1, lane, LANES), axis=-1, keepdims=True)
    rest = jnp.where(lane == i1, -jnp.inf, logits)
    m2 = jnp.max(rest, axis=-1, keepdims=True)
    i2 = jnp.min(jnp.where(rest == m2, lane, LANES), axis=-1, keepdims=True)
    e2 = jnp.exp(m2 - m1)
    denom = 1.0 + e2
    rw_ref[...] = jnp.where(lane == 0, 1.0 / denom, jnp.where(lane == 1, e2 / denom, 0.0))

    sel1 = lane == i1
    sel2 = lane == i2
    sel = jnp.where(sel1 | sel2, 1.0, 0.0)
    t_row = lax.broadcasted_iota(I32, (TM_PROJ, TM_PROJ), 0)
    t_col = lax.broadcasted_iota(I32, (TM_PROJ, TM_PROJ), 1)
    earlier = jnp.where(t_col < t_row, 1.0, 0.0).astype(BF16)
    rank = run_ref[...] + jnp.dot(earlier, sel.astype(BF16), preferred_element_type=F32)
    rank1 = jnp.sum(jnp.where(sel1, rank, 0.0), axis=-1, keepdims=True).astype(I32)
    rank2 = jnp.sum(jnp.where(sel2, rank, 0.0), axis=-1, keepdims=True).astype(I32)
    ri_ref[...] = jnp.where(lane == 0, i1, jnp.where(lane == 1, i2,
                            jnp.where(lane == 2, rank1, jnp.where(lane == 3, rank2, 0))))
    run = run_ref[...] + jnp.sum(sel, axis=0, keepdims=True)
    run_ref[...] = run
    cnt_ref[...] = jnp.broadcast_to(run, cnt_ref.shape)


def _proj_route(o, h, w_out, gain, w_router):
    n = h.shape[0]
    tok = lambda i: (i, 0)
    return pl.pallas_call(
        _proj_route_kernel,
        grid=(n // TM_PROJ,),
        in_specs=[
            pl.BlockSpec((TM_PROJ, D_MODEL), tok),
            pl.BlockSpec((TM_PROJ, D_MODEL), tok),
            _const_spec((D_MODEL, D_MODEL)),
            _const_spec((1, D_MODEL)),
            _const_spec((D_MODEL, LANES)),
        ],
        out_specs=[
            pl.BlockSpec((TM_PROJ, D_MODEL), tok),
            pl.BlockSpec((TM_PROJ * SLAB, LANES), tok),
            pl.BlockSpec((TM_PROJ, LANES), tok),
            pl.BlockSpec((TM_PROJ, LANES), tok),
            pl.BlockSpec((8, LANES), lambda i: (0, 0)),
        ],
        out_shape=[
            jax.ShapeDtypeStruct((n, D_MODEL), F32),
            jax.ShapeDtypeStruct((n * SLAB, LANES), F32),
            jax.ShapeDtypeStruct((n, LANES), I32),
            jax.ShapeDtypeStruct((n, LANES), F32),
            jax.ShapeDtypeStruct((8, LANES), F32),
        ],
        scratch_shapes=[pltpu.VMEM((1, LANES), F32)],
        compiler_params=_params("arbitrary"),
        name="proj_route",
    )(o, h, w_out, gain, w_router)


def _row_copy(src_ref, src_row, dst_ref, dst_row, sem):
    return pltpu.make_async_copy(src_ref.at[_slab_rows(src_row)], dst_ref.at[_slab_rows(dst_row)], sem)


def _dispatch_kernel(d0_ref, d1_ref, zrow_ref, hn_ref, xs_ref, zero_ref, zsem, sem):
    i = pl.program_id(0)

    def zero_fill(z):
        rows = pl.ds(pl.multiple_of(zrow_ref[z] * SLAB, SLAB), TM_GROUP * SLAB)
        return pltpu.make_async_copy(zero_ref, xs_ref.at[rows], zsem)

    @pl.when(i == 0)
    def _():
        zero_ref[...] = jnp.zeros(zero_ref.shape, F32)
        for z in range(2 * N_EXPERTS):
            @pl.when(zrow_ref[z] >= 0)
            def _():
                zero_fill(z).start()
        for z in range(2 * N_EXPERTS):
            @pl.when(zrow_ref[z] >= 0)
            def _():
                zero_fill(z).wait()

    base = i * TM_MOVE

    def issue(t, carry):
        _row_copy(hn_ref, t, xs_ref, d0_ref[base + t], sem).start(priority=0)
        _row_copy(hn_ref, t, xs_ref, d1_ref[base + t], sem).start(priority=1)
        return carry

    lax.fori_loop(0, TM_MOVE, issue, 0, unroll=8)

    def drain(t, carry):
        _row_copy(hn_ref, t, xs_ref, d0_ref[base + t], sem).wait()
        _row_copy(hn_ref, t, xs_ref, d1_ref[base + t], sem).wait()
        return carry

    lax.fori_loop(0, TM_MOVE, drain, 0, unroll=8)


def _dispatch(d0, d1, zero_rows, hn, rows):
    n = hn.shape[0] // SLAB
    return pl.pallas_call(
        _dispatch_kernel,
        grid_spec=pltpu.PrefetchScalarGridSpec(
            num_scalar_prefetch=3,
            grid=(n // TM_MOVE,),
            in_specs=[pl.BlockSpec((TM_MOVE * SLAB, LANES), lambda i, d0, d1, z: (i, 0))],
            out_specs=pl.BlockSpec(memory_space=pl.ANY),
            scratch_shapes=[
                pltpu.VMEM((TM_GROUP * SLAB, LANES), F32),
                pltpu.SemaphoreType.DMA,
                pltpu.SemaphoreType.DMA,
            ],
        ),
        out_shape=jax.ShapeDtypeStruct((rows * SLAB, LANES), F32),
        compiler_params=_params("arbitrary"),
        name="moe_dispatch",
    )(d0, d1, zero_rows, hn)


def _group_ffn_kernel(te_ref, tv_ref, x_ref, wg_ref, wu_ref, wd_ref, o_ref, xb_ref, acc_ref):
    g = pl.program_id(0)
    j = pl.program_id(1)

    @pl.when((tv_ref[g] == 0) & (j == 0))
    def _():
        o_ref[...] = jnp.zeros(o_ref.shape, F32)

    @pl.when(tv_ref[g] > 0)
    def _():
        @pl.when(j == 0)
        def _():
            for s in range(SLAB):
                xb_ref[:, s * LANES:(s + 1) * LANES] = _load_slab_cols(
                    x_ref, s, TM_GROUP).astype(BF16)
            acc_ref[...] = jnp.zeros(acc_ref.shape, F32)

        x = xb_ref[...]
        gate = jnp.dot(x, wg_ref[0], preferred_element_type=F32)
        up = jnp.dot(x, wu_ref[0], preferred_element_type=F32)
        act = (jax.nn.silu(gate) * up).astype(BF16)
        acc_ref[...] += jnp.dot(act, wd_ref[0], preferred_element_type=F32)

        @pl.when(j == pl.num_programs(1) - 1)
        def _():
            _store_slabs(o_ref, acc_ref[...])


def _group_ffn(tile_expert, tile_valid, xs, w_gate, w_up, w_down):
    rows = xs.shape[0] // SLAB
    n_chunks = FFN_EXPERT // FFN_EXPERT_CHUNK
    last = n_chunks - 1
    chunk = lambda g, j, tv: jnp.where(g % 2 == 0, j, last - j)
    row_map = lambda g, j, te, tv: (g, 0)
    return pl.pallas_call(
        _group_ffn_kernel,
        grid_spec=pltpu.PrefetchScalarGridSpec(
            num_scalar_prefetch=2,
            grid=(rows // TM_GROUP, n_chunks),
            in_specs=[
                pl.BlockSpec((TM_GROUP * SLAB, LANES), row_map),
                pl.BlockSpec((1, D_MODEL, FFN_EXPERT_CHUNK),
                             lambda g, j, te, tv: (te[g], 0, chunk(g, j, tv))),
                pl.BlockSpec((1, D_MODEL, FFN_EXPERT_CHUNK),
                             lambda g, j, te, tv: (te[g], 0, chunk(g, j, tv))),
                pl.BlockSpec((1, FFN_EXPERT_CHUNK, D_MODEL),
                             lambda g, j, te, tv: (te[g], chunk(g, j, tv), 0)),
            ],
            out_specs=pl.BlockSpec((TM_GROUP * SLAB, LANES), row_map),
            scratch_shapes=[
                pltpu.VMEM((TM_GROUP, D_MODEL), BF16),
                pltpu.VMEM((TM_GROUP, D_MODEL), F32),
            ],
        ),
        out_shape=jax.ShapeDtypeStruct((rows * SLAB, LANES), F32),
        compiler_params=_params("arbitrary", "arbitrary"),
        name="moe_group_ffn",
    )(tile_expert, tile_valid, xs, w_gate, w_up, w_down)


def _combine_kernel(d0_ref, d1_ref, h_ref, rw_ref, ys_ref, o_ref, r0_ref, r1_ref, sem):
    base = pl.program_id(0) * TM_MOVE

    def issue(t, carry):
        _row_copy(ys_ref, d0_ref[base + t], r0_ref, t, sem).start(priority=0)
        _row_copy(ys_ref, d1_ref[base + t], r1_ref, t, sem).start(priority=1)
        return carry

    lax.fori_loop(0, TM_MOVE, issue, 0, unroll=8)

    def drain(t, carry):
        _row_copy(ys_ref, d0_ref[base + t], r0_ref, t, sem).wait()
        _row_copy(ys_ref, d1_ref[base + t], r1_ref, t, sem).wait()
        return carry

    lax.fori_loop(0, TM_MOVE, drain, 0, unroll=8)
    rw = rw_ref[...]
    w1 = rw[:, 0:1]
    w2 = rw[:, 1:2]
    for s in range(SLAB):
        cols = slice(s * LANES, (s + 1) * LANES)
        o_ref[:, cols] = (h_ref[:, cols] + w1 * _load_slab_cols(r0_ref, s, TM_MOVE)
                          + w2 * _load_slab_cols(r1_ref, s, TM_MOVE))


def _combine(d0, d1, h, route_w, ys):
    n = h.shape[0]
    tok = lambda i, d0, d1: (i, 0)
    return pl.pallas_call(
        _combine_kernel,
        grid_spec=pltpu.PrefetchScalarGridSpec(
            num_scalar_prefetch=2,
            grid=(n // TM_MOVE,),
            in_specs=[
                pl.BlockSpec((TM_MOVE, D_MODEL), tok),
                pl.BlockSpec((TM_MOVE, LANES), tok),
                pl.BlockSpec(memory_space=pl.ANY),
            ],
            out_specs=pl.BlockSpec((TM_MOVE, D_MODEL), tok),
            scratch_shapes=[
                pltpu.VMEM((TM_MOVE * SLAB, LANES), F32),
                pltpu.VMEM((TM_MOVE * SLAB, LANES), F32),
                pltpu.SemaphoreType.DMA,
            ],
        ),
        out_shape=jax.ShapeDtypeStruct((n, D_MODEL), F32),
        compiler_params=_params("arbitrary"),
        name="moe_combine",
    )(d0, d1, h, route_w, ys)


def _routing_tables(route_i, counts, n):
    counts = counts.astype(I32)
    padded = (counts + TM_GROUP - 1) // TM_GROUP * TM_GROUP
    ends = jnp.cumsum(padded)
    starts = ends - padded
    d0 = starts[route_i[:, 0]] + route_i[:, 2]
    d1 = starts[route_i[:, 1]] + route_i[:, 3]
    n_tiles = TOP_K * n // TM_GROUP + N_EXPERTS
    used = ends[-1] // TM_GROUP
    g = jnp.arange(n_tiles, dtype=I32)
    first_row = jnp.minimum(g, used - 1) * TM_GROUP
    tile_expert = jnp.sum((ends[None, :] <= first_row[:, None]).astype(I32), axis=1)
    tile_valid = (g < used).astype(I32)
    tail = g[-N_EXPERTS:]
    zero_rows = jnp.concatenate([
        jnp.where(padded > 0, ends - TM_GROUP, -1),
        jnp.where(tail >= used, tail * TM_GROUP, -1)]).astype(I32)
    return d0, d1, zero_rows, tile_expert, tile_valid, n_tiles * TM_GROUP


def kernel(x, positions, l0_mix_norm, l0_sg_w_in, l0_sg_v_norm, l0_sg_w_spatial, l0_sg_b_spatial, l0_sg_w_out, l0_ffn_norm, l0_ffn_w_gate, l0_ffn_w_up, l0_ffn_w_down, l1_mix_norm, l1_da_w_qkv, l1_da_q_norm, l1_da_k_norm, l1_da_lambda_q1, l1_da_lambda_k1, l1_da_lambda_q2, l1_da_lambda_k2, l1_da_subln, l1_da_w_out, l1_moe_norm, l1_moe_w_router, l1_moe_w_gate, l1_moe_w_up, l1_moe_w_down):
    batch, seq, _ = x.shape
    n = batch * seq
    row = lambda a: a.reshape(1, -1)
    h = x.reshape(n, D_MODEL)

    h = _sg_mixer(h, row(l0_mix_norm), l0_sg_w_in.astype(BF16), row(l0_sg_v_norm),
                  l0_sg_w_spatial, l0_sg_b_spatial.T, l0_sg_w_out.astype(BF16))
    h = _dense_ffn(h, row(l0_ffn_norm), l0_ffn_w_gate.astype(BF16),
                   l0_ffn_w_up.astype(BF16), l0_ffn_w_down.astype(BF16))

    inv_freq = 1.0 / (ROPE_THETA ** (jnp.arange(0, ROT_DIM, 2, dtype=F32) / ROT_DIM))
    d = jnp.arange(HEAD_WIDTH) % DA_HEAD_DIM
    freq_lane = jnp.where(d < ROT_DIM, inv_freq[d % ROT_HALF], 0.0).reshape(1, HEAD_WIDTH)
    seg = jnp.arange(HEAD_WIDTH) // DA_HEAD_DIM
    seg_ones = (seg[:, None] == seg[None, :]).astype(BF16)
    q, k, vt = _qkv(h, positions.reshape(n, 1), row(l1_mix_norm), l1_da_w_qkv.astype(BF16),
                    freq_lane, row(jnp.tile(l1_da_q_norm, 2)), row(jnp.tile(l1_da_k_norm, 2)),
                    seg_ones, batch, seq)
    lam_terms = jnp.stack([l1_da_lambda_q1, l1_da_lambda_k1, l1_da_lambda_q2, l1_da_lambda_k2])
    o = _attention(q, k, vt, lam_terms, row(l1_da_subln), batch, seq)

    w_router = jnp.pad(l1_moe_w_router, ((0, 0), (0, LANES - N_EXPERTS)))
    h, hn, route_i, route_w, counts = _proj_route(
        o, h, l1_da_w_out.astype(BF16), row(l1_moe_norm), w_router)
    d0, d1, zero_rows, tile_expert, tile_valid, rows = _routing_tables(
        route_i, counts[0, :N_EXPERTS], n)
    xs = _dispatch(d0, d1, zero_rows, hn, rows)
    ys = _group_ffn(tile_expert, tile_valid, xs, l1_moe_w_gate.astype(BF16),
                    l1_moe_w_up.astype(BF16), l1_moe_w_down.astype(BF16))
    h = _combine(d0, d1, h, route_w, ys)
    return h.reshape(batch, seq, D_MODEL)
```

```python
import math

import jax
import jax.numpy as jnp
from jax import lax
from jax.experimental import pallas as pl
from jax.experimental.pallas import tpu as pltpu

F32 = jnp.float32
BF16 = jnp.bfloat16
I32 = jnp.int32

D_MODEL = 1024
SG_WIDTH = 2 * D_MODEL
SG_GROUPS = 8
SG_CHUNK = 128
SG_GROUP_WIDTH = SG_WIDTH // SG_GROUPS
DA_HEADS = 8
DA_HEAD_DIM = 64
HEAD_WIDTH = 2 * DA_HEAD_DIM
ROT_DIM = DA_HEAD_DIM // 4
ROT_HALF = ROT_DIM // 2
ROPE_THETA = 500000.0
FFN_DENSE = 2816
N_EXPERTS = 8
TOP_K = 2
FFN_EXPERT = 3584
NORM_EPS = 1e-6
LAMBDA_INIT = 0.8 - 0.6 * math.exp(-0.3 * 1)

LANES = 128
VMEM_LIMIT = 56 * 1024 * 1024

TM_SG = 512
TM_FFN = 512
TM_QKV = 512
TK = 1024
TQ = 1024
QC = 256
VT_ROWS = HEAD_WIDTH + 16
TM_PROJ = 512
TM_GROUP = 512
FFN_EXPERT_CHUNK = 1792
TM_MOVE = 512
SLAB = D_MODEL // LANES
assert SLAB == 8


def _slab_rows(t):
    return pl.ds(pl.multiple_of(t * SLAB, SLAB), SLAB)


def _load_slab_cols(ref, s, tokens):
    return ref[pl.ds(s, tokens, stride=SLAB), :]


def _store_slabs(ref, x):
    for s in range(SLAB):
        ref[pl.ds(s, x.shape[0], stride=SLAB), :] = x[:, s * LANES:(s + 1) * LANES]


def _rms(x, gain):
    return x * lax.rsqrt(jnp.mean(x * x, axis=-1, keepdims=True) + NORM_EPS) * gain


def _const_spec(shape):
    zeros = (0,) * len(shape)
    return pl.BlockSpec(shape, lambda *_: zeros, pipeline_mode=pl.Buffered(1))


def _params(*semantics):
    return pltpu.CompilerParams(dimension_semantics=semantics, vmem_limit_bytes=VMEM_LIMIT)


def _sg_mixer_kernel(x_ref, g_ref, win_ref, vg_ref, wsp_ref, bt_ref, wout_ref, o_ref, y_ref):
    x = x_ref[...]
    hn = _rms(x, g_ref[...]).astype(BF16)
    z = jax.nn.gelu(jnp.dot(hn, win_ref[...], preferred_element_type=F32))
    u = z[:, :SG_WIDTH]
    v = _rms(z[:, SG_WIDTH:], vg_ref[...]).astype(BF16)
    row = lax.broadcasted_iota(I32, (SG_CHUNK, SG_CHUNK), 0)
    col = lax.broadcasted_iota(I32, (SG_CHUNK, SG_CHUNK), 1)
    causal = col <= row
    for g in range(SG_GROUPS):
        w_masked = jnp.where(causal, wsp_ref[g], 0.0).astype(BF16)
        bias = bt_ref[:, g:g + 1]
        cols = slice(g * SG_GROUP_WIDTH, (g + 1) * SG_GROUP_WIDTH)
        for c in range(TM_SG // SG_CHUNK):
            rows = slice(c * SG_CHUNK, (c + 1) * SG_CHUNK)
            mixed = jnp.dot(w_masked, v[rows, cols], preferred_element_type=F32) + bias
            y_ref[rows, cols] = (u[rows, cols] * mixed).astype(BF16)
    o_ref[...] = x + jnp.dot(y_ref[...], wout_ref[...], preferred_element_type=F32)


def _sg_mixer(x, gain, w_in, v_gain, w_spatial, b_spatial_t, w_out):
    n = x.shape[0]
    tok = lambda i: (i, 0)
    return pl.pallas_call(
        _sg_mixer_kernel,
        grid=(n // TM_SG,),
        in_specs=[
            pl.BlockSpec((TM_SG, D_MODEL), tok),
            _const_spec((1, D_MODEL)),
            _const_spec((D_MODEL, 2 * SG_WIDTH)),
            _const_spec((1, SG_WIDTH)),
            _const_spec((SG_GROUPS, SG_CHUNK, SG_CHUNK)),
            _const_spec((SG_CHUNK, SG_GROUPS)),
            _const_spec((SG_WIDTH, D_MODEL)),
        ],
        out_specs=pl.BlockSpec((TM_SG, D_MODEL), tok),
        out_shape=jax.ShapeDtypeStruct((n, D_MODEL), F32),
        scratch_shapes=[pltpu.VMEM((TM_SG, SG_WIDTH), BF16)],
        compiler_params=_params("parallel"),
        name="sg_mixer",
    )(x, gain, w_in, v_gain, w_spatial, b_spatial_t, w_out)


def _dense_ffn_kernel(x_ref, g_ref, wg_ref, wu_ref, wd_ref, o_ref):
    x = x_ref[...]
    hn = _rms(x, g_ref[...]).astype(BF16)
    gate = jnp.dot(hn, wg_ref[...], preferred_element_type=F32)
    up = jnp.dot(hn, wu_ref[...], preferred_element_type=F32)
    act = (jax.nn.silu(gate) * up).astype(BF16)
    o_ref[...] = x + jnp.dot(act, wd_ref[...], preferred_element_type=F32)


def _dense_ffn(x, gain, w_gate, w_up, w_down):
    n = x.shape[0]
    tok = lambda i: (i, 0)
    return pl.pallas_call(
        _dense_ffn_kernel,
        grid=(n // TM_FFN,),
        in_specs=[
            pl.BlockSpec((TM_FFN, D_MODEL), tok),
            _const_spec((1, D_MODEL)),
            _const_spec((D_MODEL, FFN_DENSE)),
            _const_spec((D_MODEL, FFN_DENSE)),
            _const_spec((FFN_DENSE, D_MODEL)),
        ],
        out_specs=pl.BlockSpec((TM_FFN, D_MODEL), tok),
        out_shape=jax.ShapeDtypeStruct((n, D_MODEL), F32),
        compiler_params=_params("parallel"),
        name="dense_ffn",
    )(x, gain, w_gate, w_up, w_down)


def _qkv_kernel(x_ref, pos_ref, g_ref, w_ref, freq_ref, qg_ref, kg_ref, seg_ref,
                q_ref, k_ref, vt_ref):
    hn = _rms(x_ref[...], g_ref[...]).astype(BF16)
    qkv = jnp.dot(hn, w_ref[...], preferred_element_type=F32)

    ang = pos_ref[...].astype(F32) * freq_ref[...]
    cos = jnp.cos(ang)
    sin = jnp.sin(ang)
    d = lax.broadcasted_iota(I32, (1, HEAD_WIDTH), 1) % DA_HEAD_DIM
    sin_lo = jnp.where(d < ROT_HALF, -sin, 0.0)
    sin_hi = jnp.where((d >= ROT_HALF) & (d < ROT_DIM), sin, 0.0)
    seg = seg_ref[...]

    def norm_rope(xh, gain):
        ssq = jnp.dot((xh * xh).astype(BF16), seg, preferred_element_type=F32)
        xn = xh * lax.rsqrt(ssq * (1.0 / DA_HEAD_DIM) + NORM_EPS) * gain
        nxt = pltpu.roll(xn, HEAD_WIDTH - ROT_HALF, 1)
        prv = pltpu.roll(xn, ROT_HALF, 1)
        return xn * cos + nxt * sin_lo + prv * sin_hi

    scale = math.log2(math.e) / math.sqrt(DA_HEAD_DIM)
    for h in range(DA_HEADS):
        cols = slice(h * HEAD_WIDTH, (h + 1) * HEAD_WIDTH)
        q = norm_rope(qkv[:, cols], qg_ref[...])
        q_ref[:, cols] = (q * scale).astype(BF16)
        kcols = slice(D_MODEL + h * HEAD_WIDTH, D_MODEL + (h + 1) * HEAD_WIDTH)
        k_ref[:, cols] = norm_rope(qkv[:, kcols], kg_ref[...]).astype(BF16)
        vcols = slice(2 * D_MODEL + h * HEAD_WIDTH, 2 * D_MODEL + (h + 1) * HEAD_WIDTH)
        vt_ref[0, h, 0, :HEAD_WIDTH, :] = qkv[:, vcols].T.astype(BF16)
        vt_ref[0, h, 0, HEAD_WIDTH:, :] = jnp.ones((VT_ROWS - HEAD_WIDTH, TM_QKV), BF16)


def _qkv(x, pos, gain, w_qkv, freq_lane, q_gain, k_gain, seg_ones, batch, seq):
    n = x.shape[0]
    tiles = seq // TM_QKV
    per_key_tile = TK // TM_QKV
    tok = lambda i: (i, 0)
    out = jax.ShapeDtypeStruct((n, D_MODEL), BF16)
    return pl.pallas_call(
        _qkv_kernel,
        grid=(n // TM_QKV,),
        in_specs=[
            pl.BlockSpec((TM_QKV, D_MODEL), tok),
            pl.BlockSpec((TM_QKV, 1), tok),
            _const_spec((1, D_MODEL)),
            _const_spec((D_MODEL, 3 * D_MODEL)),
            _const_spec((1, HEAD_WIDTH)),
            _const_spec((1, HEAD_WIDTH)),
            _const_spec((1, HEAD_WIDTH)),
            _const_spec((HEAD_WIDTH, HEAD_WIDTH)),
        ],
        out_specs=[
            pl.BlockSpec((TM_QKV, D_MODEL), tok),
            pl.BlockSpec((TM_QKV, D_MODEL), tok),
            pl.BlockSpec((1, DA_HEADS, 1, VT_ROWS, TM_QKV),
                         lambda i: (i // tiles, 0, (i % tiles) // per_key_tile, 0,
                                    i % per_key_tile)),
        ],
        out_shape=[out, out,
                   jax.ShapeDtypeStruct((batch, DA_HEADS, seq // TK, VT_ROWS, TK), BF16)],
        compiler_params=_params("parallel"),
        name="qkv_norm_rope",
    )(x, pos, gain, w_qkv, freq_lane, q_gain, k_gain, seg_ones)


def _attn_kernel(lam_ref, sub_ref, q_ref, k_ref, vt_ref, o_ref,
                 qqt_ref, m_ref, acc_ref, sc_ref, pc_ref, ac_ref):
    qi = pl.program_id(2)
    qt = q_ref[0].astype(F32).T
    first = lax.broadcasted_iota(I32, (HEAD_WIDTH, TQ), 0) < DA_HEAD_DIM
    qqt_ref[:, :TQ] = jnp.where(first, qt, 0.0).astype(BF16)
    qqt_ref[:, TQ:] = jnp.where(first, 0.0, qt).astype(BF16)
    m_ref[...] = jnp.full(m_ref.shape, -jnp.inf, F32)
    acc_ref[...] = jnp.zeros(acc_ref.shape, F32)

    n_chunks = 2 * TQ // QC
    chunk_cols = [slice(c * QC, (c + 1) * QC) for c in range(n_chunks)]

    def scores(k, cols):
        return jnp.dot(k, qqt_ref[:, cols], preferred_element_type=F32)

    visible = (lax.broadcasted_iota(I32, (QC, QC), 0)
               <= lax.broadcasted_iota(I32, (QC, QC), 1))

    def softmax(s, cols, q_offset):
        if q_offset is not None:
            tail = jnp.where(visible, s[q_offset:], -jnp.inf)
            s = jnp.concatenate([s[:q_offset], tail], axis=0) if q_offset else tail
        m_old = m_ref[:, cols]
        m_new = jnp.maximum(m_old, jnp.max(s, axis=0, keepdims=True))
        alpha = jnp.exp2(m_old - m_new)
        p = jnp.exp2((s - m_new).astype(BF16))
        m_ref[:, cols] = m_new
        return p, alpha

    def accumulate(p, alpha, vt, cols):
        acc_ref[:, cols] = alpha * acc_ref[:, cols] + jnp.dot(
            vt, p, preferred_element_type=F32)

    def k_block(j):
        return k_ref[0, pl.ds(pl.multiple_of(j * TK, TK), TK), :]

    def full_block(j, carry):
        s_cur, p_prev, alpha_prev = sc_ref[...], pc_ref[...], ac_ref[...]
        k = k_block(j)
        vt = vt_ref[0, 0, j]
        for c in range(n_chunks):
            if c + 1 < n_chunks:
                s_next = scores(k, chunk_cols[c + 1])
            else:
                sc_ref[...] = scores(k_block(j + 1), chunk_cols[0])
            p, alpha = softmax(s_cur, chunk_cols[c], None)
            vt_prev = vt_ref[0, 0, jnp.maximum(j - 1, 0)] if c == 0 else vt
            accumulate(p_prev, alpha_prev, vt_prev, chunk_cols[c - 1])
            s_cur, p_prev, alpha_prev = s_next, p, alpha
        pc_ref[...] = p_prev
        ac_ref[...] = alpha_prev
        return carry

    sc_ref[...] = scores(k_block(0), chunk_cols[0])
    pc_ref[...] = jnp.zeros(pc_ref.shape, BF16)
    ac_ref[...] = jnp.ones(ac_ref.shape, F32)
    lax.fori_loop(0, qi, full_block, 0)

    k_diag = k_block(qi)
    vt_diag = vt_ref[0, 0, qi]
    q_offsets = [(c * QC) % TQ for c in range(n_chunks)]
    nkeys = [q0 + QC for q0 in q_offsets]
    s_cur, p_prev, alpha_prev = sc_ref[:nkeys[0], :], pc_ref[...], ac_ref[...]
    vt_prev = vt_ref[0, 0, jnp.maximum(qi - 1, 0)]
    for c in range(n_chunks):
        if c + 1 < n_chunks:
            s_next = scores(k_diag[:nkeys[c + 1]], chunk_cols[c + 1])
        p, alpha = softmax(s_cur, chunk_cols[c], q_offsets[c])
        accumulate(p_prev, alpha_prev, vt_prev, chunk_cols[c - 1])
        s_cur, p_prev, alpha_prev, vt_prev = s_next, p, alpha, vt_diag[:, :nkeys[c]]
    accumulate(p_prev, alpha_prev, vt_prev, chunk_cols[n_chunks - 1])

    lam_terms = lam_ref[...]
    lam = (jnp.exp(jnp.sum(lam_terms[0:1] * lam_terms[1:2]))
           - jnp.exp(jnp.sum(lam_terms[2:3] * lam_terms[3:4])) + LAMBDA_INIT)
    pv = acc_ref[:HEAD_WIDTH, :]
    l = acc_ref[HEAD_WIDTH:HEAD_WIDTH + 1, :]
    ot = pv[:, :TQ] / l[:, :TQ] - lam * (pv[:, TQ:] / l[:, TQ:])
    o_ref[0] = (_rms(ot.T, sub_ref[...]) * (1.0 - LAMBDA_INIT)).astype(o_ref.dtype)


def _attention(q, k, vt, lam_terms, subln, batch, seq):
    q = q.reshape(batch, seq, D_MODEL)
    k = k.reshape(batch, seq, D_MODEL)
    q_spec = pl.BlockSpec((1, TQ, HEAD_WIDTH), lambda b, h, i: (b, i, h))
    o = pl.pallas_call(
        _attn_kernel,
        grid=(batch, DA_HEADS, seq // TQ),
        in_specs=[
            _const_spec((4, DA_HEAD_DIM)),
            _const_spec((1, HEAD_WIDTH)),
            q_spec,
            pl.BlockSpec((1, seq, HEAD_WIDTH), lambda b, h, i: (b, 0, h)),
            pl.BlockSpec((1, 1, seq // TK, VT_ROWS, TK), lambda b, h, i: (b, h, 0, 0, 0)),
        ],
        out_specs=q_spec,
        out_shape=jax.ShapeDtypeStruct((batch, seq, D_MODEL), BF16),
        scratch_shapes=[
            pltpu.VMEM((HEAD_WIDTH, 2 * TQ), BF16),
            pltpu.VMEM((1, 2 * TQ), F32),
            pltpu.VMEM((VT_ROWS, 2 * TQ), F32),
            pltpu.VMEM((TK, QC), F32),
            pltpu.VMEM((TK, QC), BF16),
            pltpu.VMEM((1, QC), F32),
        ],
        compiler_params=_params("parallel", "parallel", "arbitrary"),
        name="diff_attention",
    )(lam_terms, subln, q, k, vt)
    return o.reshape(batch * seq, D_MODEL)


def _proj_route_kernel(o_ref, h_ref, wo_ref, g_ref, wrh_ref, wrl_ref,
                       h2_ref, hn_ref, ri_ref, rw_ref, cnt_ref, run_ref):
    @pl.when(pl.program_id(0) == 0)
    def _():
        run_ref[...] = jnp.zeros(run_ref.shape, F32)

    h2 = h_ref[...] + jnp.dot(o_ref[...], wo_ref[...], preferred_element_type=F32)
    h2_ref[...] = h2
    hn = _rms(h2, g_ref[...])
    _store_slabs(hn_ref, hn)
    hn_hi = hn.astype(BF16)
    hn_lo = (hn - hn_hi.astype(F32)).astype(BF16)
    logits = (jnp.dot(hn_hi, wrh_ref[...], preferred_element_type=F32)
              + jnp.dot(hn_lo, wrh_ref[...], preferred_element_type=F32)
              + jnp.dot(hn_hi, wrl_ref[...], preferred_element_type=F32))
    lane = lax.broadcasted_iota(I32, logits.shape, 1)
    logits = jnp.where(lane < N_EXPERTS, logits, -jnp.inf)
    m1 = jnp.max(logits, axis=-1, keepdims=True)
    i1 = jnp.min(jnp.where(logits == m1, lane, LANES), axis=-1, keepdims=True)
    rest = jnp.where(lane == i1, -jnp.inf, logits)
    m2 = jnp.max(rest, axis=-1, keepdims=True)
    i2 = jnp.min(jnp.where(rest == m2, lane, LANES), axis=-1, keepdims=True)
    e2 = jnp.exp(m2 - m1)
    denom = 1.0 + e2
    rw_ref[...] = jnp.where(lane == 0, 1.0 / denom, jnp.where(lane == 1, e2 / denom, 0.0))

    sel1 = lane == i1
    sel2 = lane == i2
    sel = jnp.where(sel1 | sel2, 1.0, 0.0)
    t_row = lax.broadcasted_iota(I32, (TM_PROJ, TM_PROJ), 0)
    t_col = lax.broadcasted_iota(I32, (TM_PROJ, TM_PROJ), 1)
    earlier = jnp.where(t_col < t_row, 1.0, 0.0).astype(BF16)
    rank = run_ref[...] + jnp.dot(earlier, sel.astype(BF16), preferred_element_type=F32)
    rank1 = jnp.sum(jnp.where(sel1, rank, 0.0), axis=-1, keepdims=True).astype(I32)
    rank2 = jnp.sum(jnp.where(sel2, rank, 0.0), axis=-1, keepdims=True).astype(I32)
    ri_ref[...] = jnp.where(lane == 0, i1, jnp.where(lane == 1, i2,
                            jnp.where(lane == 2, rank1, jnp.where(lane == 3, rank2, 0))))
    run = run_ref[...] + jnp.sum(sel, axis=0, keepdims=True)
    run_ref[...] = run
    cnt_ref[...] = jnp.broadcast_to(run, cnt_ref.shape)


def _proj_route(o, h, w_out, gain, w_router_hi, w_router_lo):
    n = h.shape[0]
    tok = lambda i: (i, 0)
    return pl.pallas_call(
        _proj_route_kernel,
        grid=(n // TM_PROJ,),
        in_specs=[
            pl.BlockSpec((TM_PROJ, D_MODEL), tok),
            pl.BlockSpec((TM_PROJ, D_MODEL), tok),
            _const_spec((D_MODEL, D_MODEL)),
            _const_spec((1, D_MODEL)),
            _const_spec((D_MODEL, LANES)),
            _const_spec((D_MODEL, LANES)),
        ],
        out_specs=[
            pl.BlockSpec((TM_PROJ, D_MODEL), tok),
            pl.BlockSpec((TM_PROJ * SLAB, LANES), tok),
            pl.BlockSpec((TM_PROJ, LANES), tok),
            pl.BlockSpec((TM_PROJ, LANES), tok),
            pl.BlockSpec((8, LANES), lambda i: (0, 0)),
        ],
        out_shape=[
            jax.ShapeDtypeStruct((n, D_MODEL), F32),
            jax.ShapeDtypeStruct((n * SLAB, LANES), F32),
            jax.ShapeDtypeStruct((n, LANES), I32),
            jax.ShapeDtypeStruct((n, LANES), F32),
            jax.ShapeDtypeStruct((8, LANES), F32),
        ],
        scratch_shapes=[pltpu.VMEM((1, LANES), F32)],
        compiler_params=_params("arbitrary"),
        name="proj_route",
    )(o, h, w_out, gain, w_router_hi, w_router_lo)


def _row_copy(src_ref, src_row, dst_ref, dst_row, sem):
    return pltpu.make_async_copy(src_ref.at[_slab_rows(src_row)], dst_ref.at[_slab_rows(dst_row)], sem)


def _dispatch_kernel(d0_ref, d1_ref, zrow_ref, hn_ref, xs_ref, zero_ref, zsem, sem):
    i = pl.program_id(0)

    def zero_fill(z):
        rows = pl.ds(pl.multiple_of(zrow_ref[z] * SLAB, SLAB), TM_GROUP * SLAB)
        return pltpu.make_async_copy(zero_ref, xs_ref.at[rows], zsem)

    @pl.when(i == 0)
    def _():
        zero_ref[...] = jnp.zeros(zero_ref.shape, F32)
        for z in range(2 * N_EXPERTS):
            @pl.when(zrow_ref[z] >= 0)
            def _():
                zero_fill(z).start()
        for z in range(2 * N_EXPERTS):
            @pl.when(zrow_ref[z] >= 0)
            def _():
                zero_fill(z).wait()

    base = i * TM_MOVE

    def issue(t, carry):
        _row_copy(hn_ref, t, xs_ref, d0_ref[base + t], sem).start(priority=0)
        _row_copy(hn_ref, t, xs_ref, d1_ref[base + t], sem).start(priority=1)
        return carry

    lax.fori_loop(0, TM_MOVE, issue, 0, unroll=8)

    def drain(t, carry):
        _row_copy(hn_ref, t, xs_ref, d0_ref[base + t], sem).wait()
        _row_copy(hn_ref, t, xs_ref, d1_ref[base + t], sem).wait()
        return carry

    lax.fori_loop(0, TM_MOVE, drain, 0, unroll=8)


def _dispatch(d0, d1, zero_rows, hn, rows):
    n = hn.shape[0] // SLAB
    return pl.pallas_call(
        _dispatch_kernel,
        grid_spec=pltpu.PrefetchScalarGridSpec(
            num_scalar_prefetch=3,
            grid=(n // TM_MOVE,),
            in_specs=[pl.BlockSpec((TM_MOVE * SLAB, LANES), lambda i, d0, d1, z: (i, 0))],
            out_specs=pl.BlockSpec(memory_space=pl.ANY),
            scratch_shapes=[
                pltpu.VMEM((TM_GROUP * SLAB, LANES), F32),
                pltpu.SemaphoreType.DMA,
                pltpu.SemaphoreType.DMA,
            ],
        ),
        out_shape=jax.ShapeDtypeStruct((rows * SLAB, LANES), F32),
        compiler_params=_params("arbitrary"),
        name="moe_dispatch",
    )(d0, d1, zero_rows, hn)


def _group_ffn_kernel(te_ref, tv_ref, x_ref, wg_ref, wu_ref, wd_ref, o_ref, xb_ref, acc_ref):
    g = pl.program_id(0)
    j = pl.program_id(1)

    @pl.when((tv_ref[g] == 0) & (j == 0))
    def _():
        o_ref[...] = jnp.zeros(o_ref.shape, F32)

    @pl.when(tv_ref[g] > 0)
    def _():
        @pl.when(j == 0)
        def _():
            for s in range(SLAB):
                xb_ref[:, s * LANES:(s + 1) * LANES] = _load_slab_cols(
                    x_ref, s, TM_GROUP).astype(BF16)
            acc_ref[...] = jnp.zeros(acc_ref.shape, F32)

        x = xb_ref[...]
        gate = jnp.dot(x, wg_ref[0], preferred_element_type=F32)
        up = jnp.dot(x, wu_ref[0], preferred_element_type=F32)
        act = (jax.nn.silu(gate) * up).astype(BF16)
        acc_ref[...] += jnp.dot(act, wd_ref[0], preferred_element_type=F32)

        @pl.when(j == pl.num_programs(1) - 1)
        def _():
            _store_slabs(o_ref, acc_ref[...])


def _group_ffn(tile_expert, tile_valid, xs, w_gate, w_up, w_down):
    rows = xs.shape[0] // SLAB
    n_chunks = FFN_EXPERT // FFN_EXPERT_CHUNK
    last = n_chunks - 1
    chunk = lambda g, j, tv: jnp.where(g % 2 == 0, j, last - j)
    row_map = lambda g, j, te, tv: (g, 0)
    return pl.pallas_call(
        _group_ffn_kernel,
        grid_spec=pltpu.PrefetchScalarGridSpec(
            num_scalar_prefetch=2,
            grid=(rows // TM_GROUP, n_chunks),
            in_specs=[
                pl.BlockSpec((TM_GROUP * SLAB, LANES), row_map),
                pl.BlockSpec((1, D_MODEL, FFN_EXPERT_CHUNK),
                             lambda g, j, te, tv: (te[g], 0, chunk(g, j, tv))),
                pl.BlockSpec((1, D_MODEL, FFN_EXPERT_CHUNK),
                             lambda g, j, te, tv: (te[g], 0, chunk(g, j, tv))),
                pl.BlockSpec((1, FFN_EXPERT_CHUNK, D_MODEL),
                             lambda g, j, te, tv: (te[g], chunk(g, j, tv), 0)),
            ],
            out_specs=pl.BlockSpec((TM_GROUP * SLAB, LANES), row_map),
            scratch_shapes=[
                pltpu.VMEM((TM_GROUP, D_MODEL), BF16),
                pltpu.VMEM((TM_GROUP, D_MODEL), F32),
            ],
        ),
        out_shape=jax.ShapeDtypeStruct((rows * SLAB, LANES), F32),
        compiler_params=_params("arbitrary", "arbitrary"),
        name="moe_group_ffn",
    )(tile_expert, tile_valid, xs, w_gate, w_up, w_down)


def _combine_kernel(d0_ref, d1_ref, h_ref, rw_ref, ys_ref, o_ref, r0_ref, r1_ref, sem):
    base = pl.program_id(0) * TM_MOVE

    def issue(t, carry):
        _row_copy(ys_ref, d0_ref[base + t], r0_ref, t, sem).start(priority=0)
        _row_copy(ys_ref, d1_ref[base + t], r1_ref, t, sem).start(priority=1)
        return carry

    lax.fori_loop(0, TM_MOVE, issue, 0, unroll=8)

    def drain(t, carry):
        _row_copy(ys_ref, d0_ref[base + t], r0_ref, t, sem).wait()
        _row_copy(ys_ref, d1_ref[base + t], r1_ref, t, sem).wait()
        return carry

    lax.fori_loop(0, TM_MOVE, drain, 0, unroll=8)
    rw = rw_ref[...]
    w1 = rw[:, 0:1]
    w2 = rw[:, 1:2]
    for s in range(SLAB):
        cols = slice(s * LANES, (s + 1) * LANES)
        o_ref[:, cols] = (h_ref[:, cols] + w1 * _load_slab_cols(r0_ref, s, TM_MOVE)
                          + w2 * _load_slab_cols(r1_ref, s, TM_MOVE))


def _combine(d0, d1, h, route_w, ys):
    n = h.shape[0]
    tok = lambda i, d0, d1: (i, 0)
    return pl.pallas_call(
        _combine_kernel,
        grid_spec=pltpu.PrefetchScalarGridSpec(
            num_scalar_prefetch=2,
            grid=(n // TM_MOVE,),
            in_specs=[
                pl.BlockSpec((TM_MOVE, D_MODEL), tok),
                pl.BlockSpec((TM_MOVE, LANES), tok),
                pl.BlockSpec(memory_space=pl.ANY),
            ],
            out_specs=pl.BlockSpec((TM_MOVE, D_MODEL), tok),
            scratch_shapes=[
                pltpu.VMEM((TM_MOVE * SLAB, LANES), F32),
                pltpu.VMEM((TM_MOVE * SLAB, LANES), F32),
                pltpu.SemaphoreType.DMA,
            ],
        ),
        out_shape=jax.ShapeDtypeStruct((n, D_MODEL), F32),
        compiler_params=_params("arbitrary"),
        name="moe_combine",
    )(d0, d1, h, route_w, ys)


def _routing_tables(route_i, counts, n):
    counts = counts.astype(I32)
    padded = (counts + TM_GROUP - 1) // TM_GROUP * TM_GROUP
    ends = jnp.cumsum(padded)
    starts = ends - padded
    d0 = starts[route_i[:, 0]] + route_i[:, 2]
    d1 = starts[route_i[:, 1]] + route_i[:, 3]
    n_tiles = TOP_K * n // TM_GROUP + N_EXPERTS
    used = ends[-1] // TM_GROUP
    g = jnp.arange(n_tiles, dtype=I32)
    first_row = jnp.minimum(g, used - 1) * TM_GROUP
    tile_expert = jnp.sum((ends[None, :] <= first_row[:, None]).astype(I32), axis=1)
    tile_valid = (g < used).astype(I32)
    tail = g[-N_EXPERTS:]
    zero_rows = jnp.concatenate([
        jnp.where(padded > 0, ends - TM_GROUP, -1),
        jnp.where(tail >= used, tail * TM_GROUP, -1)]).astype(I32)
    return d0, d1, zero_rows, tile_expert, tile_valid, n_tiles * TM_GROUP


def kernel(x, positions, l0_mix_norm, l0_sg_w_in, l0_sg_v_norm, l0_sg_w_spatial, l0_sg_b_spatial, l0_sg_w_out, l0_ffn_norm, l0_ffn_w_gate, l0_ffn_w_up, l0_ffn_w_down, l1_mix_norm, l1_da_w_qkv, l1_da_q_norm, l1_da_k_norm, l1_da_lambda_q1, l1_da_lambda_k1, l1_da_lambda_q2, l1_da_lambda_k2, l1_da_subln, l1_da_w_out, l1_moe_norm, l1_moe_w_router, l1_moe_w_gate, l1_moe_w_up, l1_moe_w_down):
    batch, seq, _ = x.shape
    n = batch * seq
    row = lambda a: a.reshape(1, -1)
    h = x.reshape(n, D_MODEL)

    h = _sg_mixer(h, row(l0_mix_norm), l0_sg_w_in.astype(BF16), row(l0_sg_v_norm),
                  l0_sg_w_spatial, l0_sg_b_spatial.T, l0_sg_w_out.astype(BF16))
    h = _dense_ffn(h, row(l0_ffn_norm), l0_ffn_w_gate.astype(BF16),
                   l0_ffn_w_up.astype(BF16), l0_ffn_w_down.astype(BF16))

    inv_freq = 1.0 / (ROPE_THETA ** (jnp.arange(0, ROT_DIM, 2, dtype=F32) / ROT_DIM))
    d = jnp.arange(HEAD_WIDTH) % DA_HEAD_DIM
    freq_lane = jnp.where(d < ROT_DIM, inv_freq[d % ROT_HALF], 0.0).reshape(1, HEAD_WIDTH)
    seg = jnp.arange(HEAD_WIDTH) // DA_HEAD_DIM
    seg_ones = (seg[:, None] == seg[None, :]).astype(BF16)
    q, k, vt = _qkv(h, positions.reshape(n, 1), row(l1_mix_norm), l1_da_w_qkv.astype(BF16),
                    freq_lane, row(jnp.tile(l1_da_q_norm, 2)), row(jnp.tile(l1_da_k_norm, 2)),
                    seg_ones, batch, seq)
    lam_terms = jnp.stack([l1_da_lambda_q1, l1_da_lambda_k1, l1_da_lambda_q2, l1_da_lambda_k2])
    o = _attention(q, k, vt, lam_terms, row(l1_da_subln), batch, seq)

    w_router = jnp.pad(l1_moe_w_router, ((0, 0), (0, LANES - N_EXPERTS)))
    w_router_hi = w_router.astype(BF16)
    w_router_lo = (w_router - w_router_hi.astype(F32)).astype(BF16)
    h, hn, route_i, route_w, counts = _proj_route(
        o, h, l1_da_w_out.astype(BF16), row(l1_moe_norm), w_router_hi, w_router_lo)
    d0, d1, zero_rows, tile_expert, tile_valid, rows = _routing_tables(
        route_i, counts[0, :N_EXPERTS], n)
    xs = _dispatch(d0, d1, zero_rows, hn, rows)
    ys = _group_ffn(tile_expert, tile_valid, xs, l1_moe_w_gate.astype(BF16),
                    l1_moe_w_up.astype(BF16), l1_moe_w_down.astype(BF16))
    h = _combine(d0, d1, h, route_w, ys)
    return h.reshape(batch, seq, D_MODEL)
```

```python
import math

import jax
import jax.numpy as jnp
from jax import lax
from jax.experimental import pallas as pl
from jax.experimental.pallas import tpu as pltpu

F32 = jnp.float32
BF16 = jnp.bfloat16
I32 = jnp.int32

D_MODEL = 1024
SG_WIDTH = 2 * D_MODEL
SG_GROUPS = 8
SG_CHUNK = 128
SG_GROUP_WIDTH = SG_WIDTH // SG_GROUPS
DA_HEADS = 8
DA_HEAD_DIM = 64
HEAD_WIDTH = 2 * DA_HEAD_DIM
ROT_DIM = DA_HEAD_DIM // 4
ROT_HALF = ROT_DIM // 2
ROPE_THETA = 500000.0
FFN_DENSE = 2816
N_EXPERTS = 8
TOP_K = 2
FFN_EXPERT = 3584
NORM_EPS = 1e-6
LAMBDA_INIT = 0.8 - 0.6 * math.exp(-0.3 * 1)

LANES = 128
VMEM_LIMIT = 56 * 1024 * 1024

TM_SG = 512
TM_FFN = 512
TM_QKV = 512
TK = 1024
TQ = 1024
QC = 256
VT_ROWS = HEAD_WIDTH + 16
TM_PROJ = 512
TM_GROUP = 512
FFN_EXPERT_CHUNK = 1792
TM_MOVE = 512
SLAB = D_MODEL // LANES
assert SLAB == 8


def _slab_rows(t):
    return pl.ds(pl.multiple_of(t * SLAB, SLAB), SLAB)


def _load_slab_cols(ref, s, tokens):
    return ref[pl.ds(s, tokens, stride=SLAB), :]


def _store_slabs(ref, x):
    for s in range(SLAB):
        ref[pl.ds(s, x.shape[0], stride=SLAB), :] = x[:, s * LANES:(s + 1) * LANES]


def _rms(x, gain):
    return x * lax.rsqrt(jnp.mean(x * x, axis=-1, keepdims=True) + NORM_EPS) * gain


def _const_spec(shape):
    zeros = (0,) * len(shape)
    return pl.BlockSpec(shape, lambda *_: zeros, pipeline_mode=pl.Buffered(1))


def _params(*semantics):
    return pltpu.CompilerParams(dimension_semantics=semantics, vmem_limit_bytes=VMEM_LIMIT)


def _sg_mixer_kernel(x_ref, g_ref, win_ref, vg_ref, wsp_ref, bt_ref, wout_ref, o_ref, y_ref):
    x = x_ref[...]
    hn = _rms(x, g_ref[...]).astype(BF16)
    z = jax.nn.gelu(jnp.dot(hn, win_ref[...], preferred_element_type=F32))
    u = z[:, :SG_WIDTH]
    v = _rms(z[:, SG_WIDTH:], vg_ref[...]).astype(BF16)
    row = lax.broadcasted_iota(I32, (SG_CHUNK, SG_CHUNK), 0)
    col = lax.broadcasted_iota(I32, (SG_CHUNK, SG_CHUNK), 1)
    causal = col <= row
    for g in range(SG_GROUPS):
        w_masked = jnp.where(causal, wsp_ref[g], 0.0).astype(BF16)
        bias = bt_ref[:, g:g + 1]
        cols = slice(g * SG_GROUP_WIDTH, (g + 1) * SG_GROUP_WIDTH)
        for c in range(TM_SG // SG_CHUNK):
            rows = slice(c * SG_CHUNK, (c + 1) * SG_CHUNK)
            mixed = jnp.dot(w_masked, v[rows, cols], preferred_element_type=F32) + bias
            y_ref[rows, cols] = (u[rows, cols] * mixed).astype(BF16)
    o_ref[...] = x + jnp.dot(y_ref[...], wout_ref[...], preferred_element_type=F32)


def _sg_mixer(x, gain, w_in, v_gain, w_spatial, b_spatial_t, w_out):
    n = x.shape[0]
    tok = lambda i: (i, 0)
    return pl.pallas_call(
        _sg_mixer_kernel,
        grid=(n // TM_SG,),
        in_specs=[
            pl.BlockSpec((TM_SG, D_MODEL), tok),
            _const_spec((1, D_MODEL)),
            _const_spec((D_MODEL, 2 * SG_WIDTH)),
            _const_spec((1, SG_WIDTH)),
            _const_spec((SG_GROUPS, SG_CHUNK, SG_CHUNK)),
            _const_spec((SG_CHUNK, SG_GROUPS)),
            _const_spec((SG_WIDTH, D_MODEL)),
        ],
        out_specs=pl.BlockSpec((TM_SG, D_MODEL), tok),
        out_shape=jax.ShapeDtypeStruct((n, D_MODEL), F32),
        scratch_shapes=[pltpu.VMEM((TM_SG, SG_WIDTH), BF16)],
        compiler_params=_params("parallel"),
        name="sg_mixer",
    )(x, gain, w_in, v_gain, w_spatial, b_spatial_t, w_out)


def _dense_ffn_kernel(x_ref, g_ref, wg_ref, wu_ref, wd_ref, o_ref):
    x = x_ref[...]
    hn = _rms(x, g_ref[...]).astype(BF16)
    gate = jnp.dot(hn, wg_ref[...], preferred_element_type=F32)
    up = jnp.dot(hn, wu_ref[...], preferred_element_type=F32)
    act = (jax.nn.silu(gate) * up).astype(BF16)
    o_ref[...] = x + jnp.dot(act, wd_ref[...], preferred_element_type=F32)


def _dense_ffn(x, gain, w_gate, w_up, w_down):
    n = x.shape[0]
    tok = lambda i: (i, 0)
    return pl.pallas_call(
        _dense_ffn_kernel,
        grid=(n // TM_FFN,),
        in_specs=[
            pl.BlockSpec((TM_FFN, D_MODEL), tok),
            _const_spec((1, D_MODEL)),
            _const_spec((D_MODEL, FFN_DENSE)),
            _const_spec((D_MODEL, FFN_DENSE)),
            _const_spec((FFN_DENSE, D_MODEL)),
        ],
        out_specs=pl.BlockSpec((TM_FFN, D_MODEL), tok),
        out_shape=jax.ShapeDtypeStruct((n, D_MODEL), F32),
        compiler_params=_params("parallel"),
        name="dense_ffn",
    )(x, gain, w_gate, w_up, w_down)


def _qkv_kernel(x_ref, pos_ref, g_ref, w_ref, freq_ref, qg_ref, kg_ref, seg_ref,
                q_ref, k_ref, vt_ref):
    hn = _rms(x_ref[...], g_ref[...]).astype(BF16)
    qkv = jnp.dot(hn, w_ref[...], preferred_element_type=F32)

    ang = pos_ref[...].astype(F32) * freq_ref[...]
    cos = jnp.cos(ang)
    sin = jnp.sin(ang)
    d = lax.broadcasted_iota(I32, (1, HEAD_WIDTH), 1) % DA_HEAD_DIM
    sin_lo = jnp.where(d < ROT_HALF, -sin, 0.0)
    sin_hi = jnp.where((d >= ROT_HALF) & (d < ROT_DIM), sin, 0.0)
    seg = seg_ref[...]

    def norm_rope(xh, gain):
        ssq = jnp.dot((xh * xh).astype(BF16), seg, preferred_element_type=F32)
        xn = xh * lax.rsqrt(ssq * (1.0 / DA_HEAD_DIM) + NORM_EPS) * gain
        nxt = pltpu.roll(xn, HEAD_WIDTH - ROT_HALF, 1)
        prv = pltpu.roll(xn, ROT_HALF, 1)
        return xn * cos + nxt * sin_lo + prv * sin_hi

    scale = math.log2(math.e) / math.sqrt(DA_HEAD_DIM)
    for h in range(DA_HEADS):
        cols = slice(h * HEAD_WIDTH, (h + 1) * HEAD_WIDTH)
        q = norm_rope(qkv[:, cols], qg_ref[...])
        q_ref[:, cols] = (q * scale).astype(BF16)
        kcols = slice(D_MODEL + h * HEAD_WIDTH, D_MODEL + (h + 1) * HEAD_WIDTH)
        k_ref[:, cols] = norm_rope(qkv[:, kcols], kg_ref[...]).astype(BF16)
        vcols = slice(2 * D_MODEL + h * HEAD_WIDTH, 2 * D_MODEL + (h + 1) * HEAD_WIDTH)
        vt_ref[0, h, 0, :HEAD_WIDTH, :] = qkv[:, vcols].T.astype(BF16)
        vt_ref[0, h, 0, HEAD_WIDTH:, :] = jnp.ones((VT_ROWS - HEAD_WIDTH, TM_QKV), BF16)


def _qkv(x, pos, gain, w_qkv, freq_lane, q_gain, k_gain, seg_ones, batch, seq):
    n = x.shape[0]
    tiles = seq // TM_QKV
    per_key_tile = TK // TM_QKV
    tok = lambda i: (i, 0)
    out = jax.ShapeDtypeStruct((n, D_MODEL), BF16)
    return pl.pallas_call(
        _qkv_kernel,
        grid=(n // TM_QKV,),
        in_specs=[
            pl.BlockSpec((TM_QKV, D_MODEL), tok),
            pl.BlockSpec((TM_QKV, 1), tok),
            _const_spec((1, D_MODEL)),
            _const_spec((D_MODEL, 3 * D_MODEL)),
            _const_spec((1, HEAD_WIDTH)),
            _const_spec((1, HEAD_WIDTH)),
            _const_spec((1, HEAD_WIDTH)),
            _const_spec((HEAD_WIDTH, HEAD_WIDTH)),
        ],
        out_specs=[
            pl.BlockSpec((TM_QKV, D_MODEL), tok),
            pl.BlockSpec((TM_QKV, D_MODEL), tok),
            pl.BlockSpec((1, DA_HEADS, 1, VT_ROWS, TM_QKV),
                         lambda i: (i // tiles, 0, (i % tiles) // per_key_tile, 0,
                                    i % per_key_tile)),
        ],
        out_shape=[out, out,
                   jax.ShapeDtypeStruct((batch, DA_HEADS, seq // TK, VT_ROWS, TK), BF16)],
        compiler_params=_params("parallel"),
        name="qkv_norm_rope",
    )(x, pos, gain, w_qkv, freq_lane, q_gain, k_gain, seg_ones)


def _attn_kernel(lam_ref, sub_ref, q_ref, k_ref, vt_ref, o_ref,
                 qqt_ref, m_ref, acc_ref, sc_ref, pc_ref, ac_ref):
    qi = pl.program_id(2)
    qt = q_ref[0].astype(F32).T
    first = lax.broadcasted_iota(I32, (HEAD_WIDTH, TQ), 0) < DA_HEAD_DIM
    qqt_ref[:, :TQ] = jnp.where(first, qt, 0.0).astype(BF16)
    qqt_ref[:, TQ:] = jnp.where(first, 0.0, qt).astype(BF16)
    m_ref[...] = jnp.full(m_ref.shape, -jnp.inf, F32)
    acc_ref[...] = jnp.zeros(acc_ref.shape, F32)

    n_chunks = 2 * TQ // QC
    chunk_cols = [slice(c * QC, (c + 1) * QC) for c in range(n_chunks)]

    def scores(k, cols):
        return jnp.dot(k, qqt_ref[:, cols], preferred_element_type=F32)

    visible = (lax.broadcasted_iota(I32, (QC, QC), 0)
               <= lax.broadcasted_iota(I32, (QC, QC), 1))

    def softmax(s, cols, q_offset):
        if q_offset is not None:
            tail = jnp.where(visible, s[q_offset:], -jnp.inf)
            s = jnp.concatenate([s[:q_offset], tail], axis=0) if q_offset else tail
        m_old = m_ref[:, cols]
        m_new = jnp.maximum(m_old, jnp.max(s, axis=0, keepdims=True))
        alpha = jnp.exp2(m_old - m_new)
        p = jnp.exp2((s - m_new).astype(BF16))
        m_ref[:, cols] = m_new
        return p, alpha

    def accumulate(p, alpha, vt, cols):
        acc_ref[:, cols] = alpha * acc_ref[:, cols] + jnp.dot(
            vt, p, preferred_element_type=F32)

    def k_block(j):
        return k_ref[0, pl.ds(pl.multiple_of(j * TK, TK), TK), :]

    def full_block(j, carry):
        s_cur, p_prev, alpha_prev = sc_ref[...], pc_ref[...], ac_ref[...]
        k = k_block(j)
        vt = vt_ref[0, 0, j]
        for c in range(n_chunks):
            if c + 1 < n_chunks:
                s_next = scores(k, chunk_cols[c + 1])
            else:
                sc_ref[...] = scores(k_block(j + 1), chunk_cols[0])
            p, alpha = softmax(s_cur, chunk_cols[c], None)
            vt_prev = vt_ref[0, 0, jnp.maximum(j - 1, 0)] if c == 0 else vt
            accumulate(p_prev, alpha_prev, vt_prev, chunk_cols[c - 1])
            s_cur, p_prev, alpha_prev = s_next, p, alpha
        pc_ref[...] = p_prev
        ac_ref[...] = alpha_prev
        return carry

    sc_ref[...] = scores(k_block(0), chunk_cols[0])
    pc_ref[...] = jnp.zeros(pc_ref.shape, BF16)
    ac_ref[...] = jnp.ones(ac_ref.shape, F32)
    lax.fori_loop(0, qi, full_block, 0)

    k_diag = k_block(qi)
    vt_diag = vt_ref[0, 0, qi]
    q_offsets = [(c * QC) % TQ for c in range(n_chunks)]
    nkeys = [q0 + QC for q0 in q_offsets]
    s_cur, p_prev, alpha_prev = sc_ref[:nkeys[0], :], pc_ref[...], ac_ref[...]
    vt_prev = vt_ref[0, 0, jnp.maximum(qi - 1, 0)]
    for c in range(n_chunks):
        if c + 1 < n_chunks:
            s_next = scores(k_diag[:nkeys[c + 1]], chunk_cols[c + 1])
        p, alpha = softmax(s_cur, chunk_cols[c], q_offsets[c])
        accumulate(p_prev, alpha_prev, vt_prev, chunk_cols[c - 1])
        s_cur, p_prev, alpha_prev, vt_prev = s_next, p, alpha, vt_diag[:, :nkeys[c]]
    accumulate(p_prev, alpha_prev, vt_prev, chunk_cols[n_chunks - 1])

    lam_terms = lam_ref[...]
    lam = (jnp.exp(jnp.sum(lam_terms[0:1] * lam_terms[1:2]))
           - jnp.exp(jnp.sum(lam_terms[2:3] * lam_terms[3:4])) + LAMBDA_INIT)
    pv = acc_ref[:HEAD_WIDTH, :]
    l = acc_ref[HEAD_WIDTH:HEAD_WIDTH + 1, :]
    ot = pv[:, :TQ] / l[:, :TQ] - lam * (pv[:, TQ:] / l[:, TQ:])
    o_ref[0] = (_rms(ot.T, sub_ref[...]) * (1.0 - LAMBDA_INIT)).astype(o_ref.dtype)


def _attention(q, k, vt, lam_terms, subln, batch, seq):
    q = q.reshape(batch, seq, D_MODEL)
    k = k.reshape(batch, seq, D_MODEL)
    q_spec = pl.BlockSpec((1, TQ, HEAD_WIDTH), lambda b, h, i: (b, i, h))
    o = pl.pallas_call(
        _attn_kernel,
        grid=(batch, DA_HEADS, seq // TQ),
        in_specs=[
            _const_spec((4, DA_HEAD_DIM)),
            _const_spec((1, HEAD_WIDTH)),
            q_spec,
            pl.BlockSpec((1, seq, HEAD_WIDTH), lambda b, h, i: (b, 0, h)),
            pl.BlockSpec((1, 1, seq // TK, VT_ROWS, TK), lambda b, h, i: (b, h, 0, 0, 0)),
        ],
        out_specs=q_spec,
        out_shape=jax.ShapeDtypeStruct((batch, seq, D_MODEL), BF16),
        scratch_shapes=[
            pltpu.VMEM((HEAD_WIDTH, 2 * TQ), BF16),
            pltpu.VMEM((1, 2 * TQ), F32),
            pltpu.VMEM((VT_ROWS, 2 * TQ), F32),
            pltpu.VMEM((TK, QC), F32),
            pltpu.VMEM((TK, QC), BF16),
            pltpu.VMEM((1, QC), F32),
        ],
        compiler_params=_params("parallel", "parallel", "arbitrary"),
        name="diff_attention",
    )(lam_terms, subln, q, k, vt)
    return o.reshape(batch * seq, D_MODEL)


def _proj_route_kernel(o_ref, h_ref, wo_ref, g_ref, wrh_ref, wrl_ref,
                       h2_ref, hn_ref, ri_ref, rw_ref, cnt_ref, run_ref):
    @pl.when(pl.program_id(0) == 0)
    def _():
        run_ref[...] = jnp.zeros(run_ref.shape, F32)

    h2 = h_ref[...] + jnp.dot(o_ref[...], wo_ref[...], preferred_element_type=F32)
    h2_ref[...] = h2
    hn = _rms(h2, g_ref[...])
    _store_slabs(hn_ref, hn)
    hn_hi = hn.astype(BF16)
    hn_lo = (hn - hn_hi.astype(F32)).astype(BF16)
    logits = (jnp.dot(hn_hi, wrh_ref[...], preferred_element_type=F32)
              + jnp.dot(hn_lo, wrh_ref[...], preferred_element_type=F32)
              + jnp.dot(hn_hi, wrl_ref[...], preferred_element_type=F32))
    lane = lax.broadcasted_iota(I32, logits.shape, 1)
    logits = jnp.where(lane < N_EXPERTS, logits, -jnp.inf)
    m1 = jnp.max(logits, axis=-1, keepdims=True)
    i1 = jnp.min(jnp.where(logits == m1, lane, LANES), axis=-1, keepdims=True)
    rest = jnp.where(lane == i1, -jnp.inf, logits)
    m2 = jnp.max(rest, axis=-1, keepdims=True)
    i2 = jnp.min(jnp.where(rest == m2, lane, LANES), axis=-1, keepdims=True)
    e2 = jnp.exp(m2 - m1)
    denom = 1.0 + e2
    rw_ref[...] = jnp.where(lane == 0, 1.0 / denom, jnp.where(lane == 1, e2 / denom, 0.0))

    sel1 = lane == i1
    sel2 = lane == i2
    sel = jnp.where(sel1 | sel2, 1.0, 0.0)
    t_row = lax.broadcasted_iota(I32, (TM_PROJ, TM_PROJ), 0)
    t_col = lax.broadcasted_iota(I32, (TM_PROJ, TM_PROJ), 1)
    earlier = jnp.where(t_col < t_row, 1.0, 0.0).astype(BF16)
    rank = run_ref[...] + jnp.dot(earlier, sel.astype(BF16), preferred_element_type=F32)
    rank1 = jnp.sum(jnp.where(sel1, rank, 0.0), axis=-1, keepdims=True).astype(I32)
    rank2 = jnp.sum(jnp.where(sel2, rank, 0.0), axis=-1, keepdims=True).astype(I32)
    ri_ref[...] = jnp.where(lane == 0, i1, jnp.where(lane == 1, i2,
                            jnp.where(lane == 2, rank1, jnp.where(lane == 3, rank2, 0))))
    run = run_ref[...] + jnp.sum(sel, axis=0, keepdims=True)
    run_ref[...] = run
    cnt_ref[...] = jnp.broadcast_to(run, cnt_ref.shape)


def _proj_route(o, h, w_out, gain, w_router_hi, w_router_lo):
    n = h.shape[0]
    tok = lambda i: (i, 0)
    return pl.pallas_call(
        _proj_route_kernel,
        grid=(n // TM_PROJ,),
        in_specs=[
            pl.BlockSpec((TM_PROJ, D_MODEL), tok),
            pl.BlockSpec((TM_PROJ, D_MODEL), tok),
            _const_spec((D_MODEL, D_MODEL)),
            _const_spec((1, D_MODEL)),
            _const_spec((D_MODEL, LANES)),
            _const_spec((D_MODEL, LANES)),
        ],
        out_specs=[
            pl.BlockSpec((TM_PROJ, D_MODEL), tok),
            pl.BlockSpec((TM_PROJ * SLAB, LANES), tok),
            pl.BlockSpec((TM_PROJ, LANES), tok),
            pl.BlockSpec((TM_PROJ, LANES), tok),
            pl.BlockSpec((8, LANES), lambda i: (0, 0)),
        ],
        out_shape=[
            jax.ShapeDtypeStruct((n, D_MODEL), F32),
            jax.ShapeDtypeStruct((n * SLAB, LANES), F32),
            jax.ShapeDtypeStruct((n, LANES), I32),
            jax.ShapeDtypeStruct((n, LANES), F32),
            jax.ShapeDtypeStruct((8, LANES), F32),
        ],
        scratch_shapes=[pltpu.VMEM((1, LANES), F32)],
        compiler_params=_params("arbitrary"),
        name="proj_route",
    )(o, h, w_out, gain, w_router_hi, w_router_lo)


def _row_copy(src_ref, src_row, dst_ref, dst_row, sem):
    return pltpu.make_async_copy(src_ref.at[_slab_rows(src_row)], dst_ref.at[_slab_rows(dst_row)], sem)


def _dispatch_kernel(d0_ref, d1_ref, zrow_ref, hn_ref, xs_ref, zero_ref, zsem, sem):
    i = pl.program_id(0)

    def zero_fill(z):
        rows = pl.ds(pl.multiple_of(zrow_ref[z] * SLAB, SLAB), TM_GROUP * SLAB)
        return pltpu.make_async_copy(zero_ref, xs_ref.at[rows], zsem)

    @pl.when(i == 0)
    def _():
        zero_ref[...] = jnp.zeros(zero_ref.shape, F32)
        for z in range(2 * N_EXPERTS):
            @pl.when(zrow_ref[z] >= 0)
            def _():
                zero_fill(z).start()
        for z in range(2 * N_EXPERTS):
            @pl.when(zrow_ref[z] >= 0)
            def _():
                zero_fill(z).wait()

    base = i * TM_MOVE

    def issue(t, carry):
        _row_copy(hn_ref, t, xs_ref, d0_ref[base + t], sem).start(priority=0)
        _row_copy(hn_ref, t, xs_ref, d1_ref[base + t], sem).start(priority=1)
        return carry

    lax.fori_loop(0, TM_MOVE, issue, 0, unroll=8)

    def drain(t, carry):
        _row_copy(hn_ref, t, xs_ref, d0_ref[base + t], sem).wait()
        _row_copy(hn_ref, t, xs_ref, d1_ref[base + t], sem).wait()
        return carry

    lax.fori_loop(0, TM_MOVE, drain, 0, unroll=8)


def _dispatch(d0, d1, zero_rows, hn, rows):
    n = hn.shape[0] // SLAB
    return pl.pallas_call(
        _dispatch_kernel,
        grid_spec=pltpu.PrefetchScalarGridSpec(
            num_scalar_prefetch=3,
            grid=(n // TM_MOVE,),
            in_specs=[pl.BlockSpec((TM_MOVE * SLAB, LANES), lambda i, d0, d1, z: (i, 0))],
            out_specs=pl.BlockSpec(memory_space=pl.ANY),
            scratch_shapes=[
                pltpu.VMEM((TM_GROUP * SLAB, LANES), F32),
                pltpu.SemaphoreType.DMA,
                pltpu.SemaphoreType.DMA,
            ],
        ),
        out_shape=jax.ShapeDtypeStruct((rows * SLAB, LANES), F32),
        compiler_params=_params("arbitrary"),
        name="moe_dispatch",
    )(d0, d1, zero_rows, hn)


def _group_ffn_kernel(te_ref, tv_ref, x_ref, wg_ref, wu_ref, wd_ref, o_ref, xb_ref, acc_ref):
    g = pl.program_id(0)
    j = pl.program_id(1)

    @pl.when((tv_ref[g] == 0) & (j == 0))
    def _():
        o_ref[...] = jnp.zeros(o_ref.shape, F32)

    @pl.when(tv_ref[g] > 0)
    def _():
        @pl.when(j == 0)
        def _():
            for s in range(SLAB):
                xb_ref[:, s * LANES:(s + 1) * LANES] = _load_slab_cols(
                    x_ref, s, TM_GROUP).astype(BF16)
            acc_ref[...] = jnp.zeros(acc_ref.shape, F32)

        x = xb_ref[...]
        gate = jnp.dot(x, wg_ref[0], preferred_element_type=F32)
        up = jnp.dot(x, wu_ref[0], preferred_element_type=F32)
        act = (jax.nn.silu(gate) * up).astype(BF16)
        acc_ref[...] += jnp.dot(act, wd_ref[0], preferred_element_type=F32)

        @pl.when(j == pl.num_programs(1) - 1)
        def _():
            _store_slabs(o_ref, acc_ref[...])


def _group_ffn(tile_expert, tile_valid, xs, w_gate, w_up, w_down):
    rows = xs.shape[0] // SLAB
    n_chunks = FFN_EXPERT // FFN_EXPERT_CHUNK
    last = n_chunks - 1
    chunk = lambda g, j, tv: jnp.where(g % 2 == 0, j, last - j)
    row_map = lambda g, j, te, tv: (g, 0)
    return pl.pallas_call(
        _group_ffn_kernel,
        grid_spec=pltpu.PrefetchScalarGridSpec(
            num_scalar_prefetch=2,
            grid=(rows // TM_GROUP, n_chunks),
            in_specs=[
                pl.BlockSpec((TM_GROUP * SLAB, LANES), row_map),
                pl.BlockSpec((1, D_MODEL, FFN_EXPERT_CHUNK),
                             lambda g, j, te, tv: (te[g], 0, chunk(g, j, tv))),
                pl.BlockSpec((1, D_MODEL, FFN_EXPERT_CHUNK),
                             lambda g, j, te, tv: (te[g], 0, chunk(g, j, tv))),
                pl.BlockSpec((1, FFN_EXPERT_CHUNK, D_MODEL),
                             lambda g, j, te, tv: (te[g], chunk(g, j, tv), 0)),
            ],
            out_specs=pl.BlockSpec((TM_GROUP * SLAB, LANES), row_map),
            scratch_shapes=[
                pltpu.VMEM((TM_GROUP, D_MODEL), BF16),
                pltpu.VMEM((TM_GROUP, D_MODEL), F32),
            ],
        ),
        out_shape=jax.ShapeDtypeStruct((rows * SLAB, LANES), F32),
        compiler_params=_params("arbitrary", "arbitrary"),
        name="moe_group_ffn",
    )(tile_expert, tile_valid, xs, w_gate, w_up, w_down)


def _combine_kernel(d0_ref, d1_ref, h_ref, rw_ref, ys_ref, o_ref, r0_ref, r1_ref, sems):
    i = pl.program_id(0)
    slot = i % 2

    def gather(tile, buf, action):
        base = tile * TM_MOVE
        r0, r1, sem = r0_ref.at[buf], r1_ref.at[buf], sems.at[buf]

        def body(t, carry):
            c0 = _row_copy(ys_ref, d0_ref[base + t], r0, t, sem)
            c1 = _row_copy(ys_ref, d1_ref[base + t], r1, t, sem)
            if action == "start":
                c0.start(priority=0)
                c1.start(priority=1)
            else:
                c0.wait()
                c1.wait()
            return carry

        lax.fori_loop(0, TM_MOVE, body, 0, unroll=8)

    @pl.when(i == 0)
    def _():
        gather(0, 0, "start")

    @pl.when(i + 1 < pl.num_programs(0))
    def _():
        gather(i + 1, 1 - slot, "start")

    gather(i, slot, "wait")
    rw = rw_ref[...]
    w1 = rw[:, 0:1]
    w2 = rw[:, 1:2]
    r0, r1 = r0_ref.at[slot], r1_ref.at[slot]
    for s in range(SLAB):
        cols = slice(s * LANES, (s + 1) * LANES)
        o_ref[:, cols] = (h_ref[:, cols] + w1 * _load_slab_cols(r0, s, TM_MOVE)
                          + w2 * _load_slab_cols(r1, s, TM_MOVE))


def _combine(d0, d1, h, route_w, ys):
    n = h.shape[0]
    tok = lambda i, d0, d1: (i, 0)
    return pl.pallas_call(
        _combine_kernel,
        grid_spec=pltpu.PrefetchScalarGridSpec(
            num_scalar_prefetch=2,
            grid=(n // TM_MOVE,),
            in_specs=[
                pl.BlockSpec((TM_MOVE, D_MODEL), tok),
                pl.BlockSpec((TM_MOVE, LANES), tok),
                pl.BlockSpec(memory_space=pl.ANY),
            ],
            out_specs=pl.BlockSpec((TM_MOVE, D_MODEL), tok),
            scratch_shapes=[
                pltpu.VMEM((2, TM_MOVE * SLAB, LANES), F32),
                pltpu.VMEM((2, TM_MOVE * SLAB, LANES), F32),
                pltpu.SemaphoreType.DMA((2,)),
            ],
        ),
        out_shape=jax.ShapeDtypeStruct((n, D_MODEL), F32),
        compiler_params=_params("arbitrary"),
        name="moe_combine",
    )(d0, d1, h, route_w, ys)


def _routing_tables(route_i, counts, n):
    counts = counts.astype(I32)
    padded = (counts + TM_GROUP - 1) // TM_GROUP * TM_GROUP
    ends = jnp.cumsum(padded)
    starts = ends - padded
    d0 = starts[route_i[:, 0]] + route_i[:, 2]
    d1 = starts[route_i[:, 1]] + route_i[:, 3]
    n_tiles = TOP_K * n // TM_GROUP + N_EXPERTS
    used = ends[-1] // TM_GROUP
    g = jnp.arange(n_tiles, dtype=I32)
    first_row = jnp.minimum(g, used - 1) * TM_GROUP
    tile_expert = jnp.sum((ends[None, :] <= first_row[:, None]).astype(I32), axis=1)
    tile_valid = (g < used).astype(I32)
    tail = g[-N_EXPERTS:]
    zero_rows = jnp.concatenate([
        jnp.where(padded > 0, ends - TM_GROUP, -1),
        jnp.where(tail >= used, tail * TM_GROUP, -1)]).astype(I32)
    return d0, d1, zero_rows, tile_expert, tile_valid, n_tiles * TM_GROUP


def kernel(x, positions, l0_mix_norm, l0_sg_w_in, l0_sg_v_norm, l0_sg_w_spatial, l0_sg_b_spatial, l0_sg_w_out, l0_ffn_norm, l0_ffn_w_gate, l0_ffn_w_up, l0_ffn_w_down, l1_mix_norm, l1_da_w_qkv, l1_da_q_norm, l1_da_k_norm, l1_da_lambda_q1, l1_da_lambda_k1, l1_da_lambda_q2, l1_da_lambda_k2, l1_da_subln, l1_da_w_out, l1_moe_norm, l1_moe_w_router, l1_moe_w_gate, l1_moe_w_up, l1_moe_w_down):
    batch, seq, _ = x.shape
    n = batch * seq
    row = lambda a: a.reshape(1, -1)
    h = x.reshape(n, D_MODEL)

    h = _sg_mixer(h, row(l0_mix_norm), l0_sg_w_in.astype(BF16), row(l0_sg_v_norm),
                  l0_sg_w_spatial, l0_sg_b_spatial.T, l0_sg_w_out.astype(BF16))
    h = _dense_ffn(h, row(l0_ffn_norm), l0_ffn_w_gate.astype(BF16),
                   l0_ffn_w_up.astype(BF16), l0_ffn_w_down.astype(BF16))

    inv_freq = 1.0 / (ROPE_THETA ** (jnp.arange(0, ROT_DIM, 2, dtype=F32) / ROT_DIM))
    d = jnp.arange(HEAD_WIDTH) % DA_HEAD_DIM
    freq_lane = jnp.where(d < ROT_DIM, inv_freq[d % ROT_HALF], 0.0).reshape(1, HEAD_WIDTH)
    seg = jnp.arange(HEAD_WIDTH) // DA_HEAD_DIM
    seg_ones = (seg[:, None] == seg[None, :]).astype(BF16)
    q, k, vt = _qkv(h, positions.reshape(n, 1), row(l1_mix_norm), l1_da_w_qkv.astype(BF16),
                    freq_lane, row(jnp.tile(l1_da_q_norm, 2)), row(jnp.tile(l1_da_k_norm, 2)),
                    seg_ones, batch, seq)
    lam_terms = jnp.stack([l1_da_lambda_q1, l1_da_lambda_k1, l1_da_lambda_q2, l1_da_lambda_k2])
    o = _attention(q, k, vt, lam_terms, row(l1_da_subln), batch, seq)

    w_router = jnp.pad(l1_moe_w_router, ((0, 0), (0, LANES - N_EXPERTS)))
    w_router_hi = w_router.astype(BF16)
    w_router_lo = (w_router - w_router_hi.astype(F32)).astype(BF16)
    h, hn, route_i, route_w, counts = _proj_route(
        o, h, l1_da_w_out.astype(BF16), row(l1_moe_norm), w_router_hi, w_router_lo)
    d0, d1, zero_rows, tile_expert, tile_valid, rows = _routing_tables(
        route_i, counts[0, :N_EXPERTS], n)
    xs = _dispatch(d0, d1, zero_rows, hn, rows)
    ys = _group_ffn(tile_expert, tile_valid, xs, l1_moe_w_gate.astype(BF16),
                    l1_moe_w_up.astype(BF16), l1_moe_w_down.astype(BF16))
    h = _combine(d0, d1, h, route_w, ys)
    return h.reshape(batch, seq, D_MODEL)
```

```python
import math

import jax
import jax.numpy as jnp
from jax import lax
from jax.experimental import pallas as pl
from jax.experimental.pallas import tpu as pltpu

F32 = jnp.float32
BF16 = jnp.bfloat16
I32 = jnp.int32

D_MODEL = 1024
SG_WIDTH = 2 * D_MODEL
SG_GROUPS = 8
SG_CHUNK = 128
SG_GROUP_WIDTH = SG_WIDTH // SG_GROUPS
DA_HEADS = 8
DA_HEAD_DIM = 64
HEAD_WIDTH = 2 * DA_HEAD_DIM
ROT_DIM = DA_HEAD_DIM // 4
ROT_HALF = ROT_DIM // 2
ROPE_THETA = 500000.0
FFN_DENSE = 2816
N_EXPERTS = 8
TOP_K = 2
FFN_EXPERT = 3584
NORM_EPS = 1e-6
LAMBDA_INIT = 0.8 - 0.6 * math.exp(-0.3 * 1)

LANES = 128
VMEM_LIMIT = 56 * 1024 * 1024

TM_SG = 512
TM_FFN = 512
TM_QKV = 512
TK = 1024
TQ = 1024
QC = 256
VT_ROWS = HEAD_WIDTH + 16
TM_PROJ = 512
TM_GROUP = 512
FFN_EXPERT_CHUNK = 1792
TM_MOVE = 512
SLAB = D_MODEL // LANES
assert SLAB == 8


def _slab_rows(t):
    return pl.ds(pl.multiple_of(t * SLAB, SLAB), SLAB)


def _load_slab_cols(ref, s, tokens):
    return ref[pl.ds(s, tokens, stride=SLAB), :]


def _store_slabs(ref, x):
    for s in range(SLAB):
        ref[pl.ds(s, x.shape[0], stride=SLAB), :] = x[:, s * LANES:(s + 1) * LANES]


def _rms(x, gain):
    return x * lax.rsqrt(jnp.mean(x * x, axis=-1, keepdims=True) + NORM_EPS) * gain


def _const_spec(shape):
    zeros = (0,) * len(shape)
    return pl.BlockSpec(shape, lambda *_: zeros, pipeline_mode=pl.Buffered(1))


def _params(*semantics):
    return pltpu.CompilerParams(dimension_semantics=semantics, vmem_limit_bytes=VMEM_LIMIT)


def _sg_mixer_kernel(x_ref, g_ref, win_ref, vg_ref, wsp_ref, bt_ref, wout_ref, o_ref, y_ref):
    x = x_ref[...]
    hn = _rms(x, g_ref[...]).astype(BF16)
    z = jax.nn.gelu(jnp.dot(hn, win_ref[...], preferred_element_type=F32))
    u = z[:, :SG_WIDTH]
    v = _rms(z[:, SG_WIDTH:], vg_ref[...]).astype(BF16)
    row = lax.broadcasted_iota(I32, (SG_CHUNK, SG_CHUNK), 0)
    col = lax.broadcasted_iota(I32, (SG_CHUNK, SG_CHUNK), 1)
    causal = col <= row
    for g in range(SG_GROUPS):
        w_masked = jnp.where(causal, wsp_ref[g], 0.0).astype(BF16)
        bias = bt_ref[:, g:g + 1]
        cols = slice(g * SG_GROUP_WIDTH, (g + 1) * SG_GROUP_WIDTH)
        for c in range(TM_SG // SG_CHUNK):
            rows = slice(c * SG_CHUNK, (c + 1) * SG_CHUNK)
            mixed = jnp.dot(w_masked, v[rows, cols], preferred_element_type=F32) + bias
            y_ref[rows, cols] = (u[rows, cols] * mixed).astype(BF16)
    o_ref[...] = x + jnp.dot(y_ref[...], wout_ref[...], preferred_element_type=F32)


def _sg_mixer(x, gain, w_in, v_gain, w_spatial, b_spatial_t, w_out):
    n = x.shape[0]
    tok = lambda i: (i, 0)
    return pl.pallas_call(
        _sg_mixer_kernel,
        grid=(n // TM_SG,),
        in_specs=[
            pl.BlockSpec((TM_SG, D_MODEL), tok),
            _const_spec((1, D_MODEL)),
            _const_spec((D_MODEL, 2 * SG_WIDTH)),
            _const_spec((1, SG_WIDTH)),
            _const_spec((SG_GROUPS, SG_CHUNK, SG_CHUNK)),
            _const_spec((SG_CHUNK, SG_GROUPS)),
            _const_spec((SG_WIDTH, D_MODEL)),
        ],
        out_specs=pl.BlockSpec((TM_SG, D_MODEL), tok),
        out_shape=jax.ShapeDtypeStruct((n, D_MODEL), F32),
        scratch_shapes=[pltpu.VMEM((TM_SG, SG_WIDTH), BF16)],
        compiler_params=_params("parallel"),
        name="sg_mixer",
    )(x, gain, w_in, v_gain, w_spatial, b_spatial_t, w_out)


def _dense_ffn_kernel(x_ref, g_ref, wg_ref, wu_ref, wd_ref, o_ref):
    x = x_ref[...]
    hn = _rms(x, g_ref[...]).astype(BF16)
    gate = jnp.dot(hn, wg_ref[...], preferred_element_type=F32)
    up = jnp.dot(hn, wu_ref[...], preferred_element_type=F32)
    act = (jax.nn.silu(gate) * up).astype(BF16)
    o_ref[...] = x + jnp.dot(act, wd_ref[...], preferred_element_type=F32)


def _dense_ffn(x, gain, w_gate, w_up, w_down):
    n = x.shape[0]
    tok = lambda i: (i, 0)
    return pl.pallas_call(
        _dense_ffn_kernel,
        grid=(n // TM_FFN,),
        in_specs=[
            pl.BlockSpec((TM_FFN, D_MODEL), tok),
            _const_spec((1, D_MODEL)),
            _const_spec((D_MODEL, FFN_DENSE)),
            _const_spec((D_MODEL, FFN_DENSE)),
            _const_spec((FFN_DENSE, D_MODEL)),
        ],
        out_specs=pl.BlockSpec((TM_FFN, D_MODEL), tok),
        out_shape=jax.ShapeDtypeStruct((n, D_MODEL), F32),
        compiler_params=_params("parallel"),
        name="dense_ffn",
    )(x, gain, w_gate, w_up, w_down)


def _qkv_kernel(x_ref, pos_ref, g_ref, w_ref, freq_ref, qg_ref, kg_ref, seg_ref,
                q_ref, k_ref, vt_ref):
    hn = _rms(x_ref[...], g_ref[...]).astype(BF16)
    qkv = jnp.dot(hn, w_ref[...], preferred_element_type=F32)

    ang = pos_ref[...].astype(F32) * freq_ref[...]
    cos = jnp.cos(ang)
    sin = jnp.sin(ang)
    d = lax.broadcasted_iota(I32, (1, HEAD_WIDTH), 1) % DA_HEAD_DIM
    sin_lo = jnp.where(d < ROT_HALF, -sin, 0.0)
    sin_hi = jnp.where((d >= ROT_HALF) & (d < ROT_DIM), sin, 0.0)
    seg = seg_ref[...]

    def norm_rope(xh, gain):
        ssq = jnp.dot((xh * xh).astype(BF16), seg, preferred_element_type=F32)
        xn = xh * lax.rsqrt(ssq * (1.0 / DA_HEAD_DIM) + NORM_EPS) * gain
        nxt = pltpu.roll(xn, HEAD_WIDTH - ROT_HALF, 1)
        prv = pltpu.roll(xn, ROT_HALF, 1)
        return xn * cos + nxt * sin_lo + prv * sin_hi

    scale = math.log2(math.e) / math.sqrt(DA_HEAD_DIM)
    for h in range(DA_HEADS):
        cols = slice(h * HEAD_WIDTH, (h + 1) * HEAD_WIDTH)
        q = norm_rope(qkv[:, cols], qg_ref[...])
        q_ref[:, cols] = (q * scale).astype(BF16)
        kcols = slice(D_MODEL + h * HEAD_WIDTH, D_MODEL + (h + 1) * HEAD_WIDTH)
        k_ref[:, cols] = norm_rope(qkv[:, kcols], kg_ref[...]).astype(BF16)
        vcols = slice(2 * D_MODEL + h * HEAD_WIDTH, 2 * D_MODEL + (h + 1) * HEAD_WIDTH)
        vt_ref[0, h, 0, :HEAD_WIDTH, :] = qkv[:, vcols].T.astype(BF16)
        vt_ref[0, h, 0, HEAD_WIDTH:, :] = jnp.ones((VT_ROWS - HEAD_WIDTH, TM_QKV), BF16)


def _qkv(x, pos, gain, w_qkv, freq_lane, q_gain, k_gain, seg_ones, batch, seq):
    n = x.shape[0]
    tiles = seq // TM_QKV
    per_key_tile = TK // TM_QKV
    tok = lambda i: (i, 0)
    out = jax.ShapeDtypeStruct((n, D_MODEL), BF16)
    return pl.pallas_call(
        _qkv_kernel,
        grid=(n // TM_QKV,),
        in_specs=[
            pl.BlockSpec((TM_QKV, D_MODEL), tok),
            pl.BlockSpec((TM_QKV, 1), tok),
            _const_spec((1, D_MODEL)),
            _const_spec((D_MODEL, 3 * D_MODEL)),
            _const_spec((1, HEAD_WIDTH)),
            _const_spec((1, HEAD_WIDTH)),
            _const_spec((1, HEAD_WIDTH)),
            _const_spec((HEAD_WIDTH, HEAD_WIDTH)),
        ],
        out_specs=[
            pl.BlockSpec((TM_QKV, D_MODEL), tok),
            pl.BlockSpec((TM_QKV, D_MODEL), tok),
            pl.BlockSpec((1, DA_HEADS, 1, VT_ROWS, TM_QKV),
                         lambda i: (i // tiles, 0, (i % tiles) // per_key_tile, 0,
                                    i % per_key_tile)),
        ],
        out_shape=[out, out,
                   jax.ShapeDtypeStruct((batch, DA_HEADS, seq // TK, VT_ROWS, TK), BF16)],
        compiler_params=_params("parallel"),
        name="qkv_norm_rope",
    )(x, pos, gain, w_qkv, freq_lane, q_gain, k_gain, seg_ones)


def _attn_kernel(lam_ref, sub_ref, q_ref, k_ref, vt_ref, o_ref,
                 qqt_ref, m_ref, acc_ref, sc_ref, pc_ref, ac_ref):
    qi = pl.program_id(2)
    qt = q_ref[0].astype(F32).T
    first = lax.broadcasted_iota(I32, (HEAD_WIDTH, TQ), 0) < DA_HEAD_DIM
    qqt_ref[:, :TQ] = jnp.where(first, qt, 0.0).astype(BF16)
    qqt_ref[:, TQ:] = jnp.where(first, 0.0, qt).astype(BF16)
    m_ref[...] = jnp.full(m_ref.shape, -jnp.inf, F32)
    acc_ref[...] = jnp.zeros(acc_ref.shape, F32)

    n_chunks = 2 * TQ // QC
    chunk_cols = [slice(c * QC, (c + 1) * QC) for c in range(n_chunks)]

    def scores(k, cols):
        return jnp.dot(k, qqt_ref[:, cols], preferred_element_type=F32)

    visible = (lax.broadcasted_iota(I32, (QC, QC), 0)
               <= lax.broadcasted_iota(I32, (QC, QC), 1))

    def softmax(s, cols, q_offset):
        if q_offset is not None:
            tail = jnp.where(visible, s[q_offset:], -jnp.inf)
            s = jnp.concatenate([s[:q_offset], tail], axis=0) if q_offset else tail
        m_old = m_ref[:, cols]
        m_new = jnp.maximum(m_old, jnp.max(s, axis=0, keepdims=True))
        alpha = jnp.exp2(m_old - m_new)
        p = jnp.exp2((s - m_new).astype(BF16))
        m_ref[:, cols] = m_new
        return p, alpha

    def accumulate(p, alpha, vt, cols):
        acc_ref[:, cols] = alpha * acc_ref[:, cols] + jnp.dot(
            vt, p, preferred_element_type=F32)

    def k_block(j):
        return k_ref[0, pl.ds(pl.multiple_of(j * TK, TK), TK), :]

    def full_block(j, carry):
        s_cur, p_prev, alpha_prev = sc_ref[...], pc_ref[...], ac_ref[...]
        k = k_block(j)
        vt = vt_ref[0, 0, j]
        for c in range(n_chunks):
            if c + 1 < n_chunks:
                s_next = scores(k, chunk_cols[c + 1])
            else:
                sc_ref[...] = scores(k_block(j + 1), chunk_cols[0])
            p, alpha = softmax(s_cur, chunk_cols[c], None)
            vt_prev = vt_ref[0, 0, jnp.maximum(j - 1, 0)] if c == 0 else vt
            accumulate(p_prev, alpha_prev, vt_prev, chunk_cols[c - 1])
            s_cur, p_prev, alpha_prev = s_next, p, alpha
        pc_ref[...] = p_prev
        ac_ref[...] = alpha_prev
        return carry

    sc_ref[...] = scores(k_block(0), chunk_cols[0])
    pc_ref[...] = jnp.zeros(pc_ref.shape, BF16)
    ac_ref[...] = jnp.ones(ac_ref.shape, F32)
    lax.fori_loop(0, qi, full_block, 0)

    k_diag = k_block(qi)
    vt_diag = vt_ref[0, 0, qi]
    q_offsets = [(c * QC) % TQ for c in range(n_chunks)]
    nkeys = [q0 + QC for q0 in q_offsets]
    s_cur, p_prev, alpha_prev = sc_ref[:nkeys[0], :], pc_ref[...], ac_ref[...]
    vt_prev = vt_ref[0, 0, jnp.maximum(qi - 1, 0)]
    for c in range(n_chunks):
        if c + 1 < n_chunks:
            s_next = scores(k_diag[:nkeys[c + 1]], chunk_cols[c + 1])
        p, alpha = softmax(s_cur, chunk_cols[c], q_offsets[c])
        accumulate(p_prev, alpha_prev, vt_prev, chunk_cols[c - 1])
        s_cur, p_prev, alpha_prev, vt_prev = s_next, p, alpha, vt_diag[:, :nkeys[c]]
    accumulate(p_prev, alpha_prev, vt_prev, chunk_cols[n_chunks - 1])

    lam_terms = lam_ref[...]
    lam = (jnp.exp(jnp.sum(lam_terms[0:1] * lam_terms[1:2]))
           - jnp.exp(jnp.sum(lam_terms[2:3] * lam_terms[3:4])) + LAMBDA_INIT)
    pv = acc_ref[:HEAD_WIDTH, :]
    l = acc_ref[HEAD_WIDTH:HEAD_WIDTH + 1, :]
    ot = pv[:, :TQ] / l[:, :TQ] - lam * (pv[:, TQ:] / l[:, TQ:])
    o_ref[0] = (_rms(ot.T, sub_ref[...]) * (1.0 - LAMBDA_INIT)).astype(o_ref.dtype)


def _attention(q, k, vt, lam_terms, subln, batch, seq):
    q = q.reshape(batch, seq, D_MODEL)
    k = k.reshape(batch, seq, D_MODEL)
    q_spec = pl.BlockSpec((1, TQ, HEAD_WIDTH), lambda b, h, i: (b, i, h))
    o = pl.pallas_call(
        _attn_kernel,
        grid=(batch, DA_HEADS, seq // TQ),
        in_specs=[
            _const_spec((4, DA_HEAD_DIM)),
            _const_spec((1, HEAD_WIDTH)),
            q_spec,
            pl.BlockSpec((1, seq, HEAD_WIDTH), lambda b, h, i: (b, 0, h)),
            pl.BlockSpec((1, 1, seq // TK, VT_ROWS, TK), lambda b, h, i: (b, h, 0, 0, 0)),
        ],
        out_specs=q_spec,
        out_shape=jax.ShapeDtypeStruct((batch, seq, D_MODEL), BF16),
        scratch_shapes=[
            pltpu.VMEM((HEAD_WIDTH, 2 * TQ), BF16),
            pltpu.VMEM((1, 2 * TQ), F32),
            pltpu.VMEM((VT_ROWS, 2 * TQ), F32),
            pltpu.VMEM((TK, QC), F32),
            pltpu.VMEM((TK, QC), BF16),
            pltpu.VMEM((1, QC), F32),
        ],
        compiler_params=_params("parallel", "parallel", "arbitrary"),
        name="diff_attention",
    )(lam_terms, subln, q, k, vt)
    return o.reshape(batch * seq, D_MODEL)


def _proj_route_kernel(o_ref, h_ref, wo_ref, g_ref, wrh_ref, wrl_ref,
                       h2_ref, hn_ref, ri_ref, rw_ref, cnt_ref, run_ref):
    @pl.when(pl.program_id(0) == 0)
    def _():
        run_ref[...] = jnp.zeros(run_ref.shape, F32)

    h2 = h_ref[...] + jnp.dot(o_ref[...], wo_ref[...], preferred_element_type=F32)
    h2_ref[...] = h2
    hn = _rms(h2, g_ref[...])
    _store_slabs(hn_ref, hn)
    hn_hi = hn.astype(BF16)
    hn_lo = (hn - hn_hi.astype(F32)).astype(BF16)
    logits = (jnp.dot(hn_hi, wrh_ref[...], preferred_element_type=F32)
              + jnp.dot(hn_lo, wrh_ref[...], preferred_element_type=F32)
              + jnp.dot(hn_hi, wrl_ref[...], preferred_element_type=F32))
    lane = lax.broadcasted_iota(I32, logits.shape, 1)
    logits = jnp.where(lane < N_EXPERTS, logits, -jnp.inf)
    m1 = jnp.max(logits, axis=-1, keepdims=True)
    i1 = jnp.min(jnp.where(logits == m1, lane, LANES), axis=-1, keepdims=True)
    rest = jnp.where(lane == i1, -jnp.inf, logits)
    m2 = jnp.max(rest, axis=-1, keepdims=True)
    i2 = jnp.min(jnp.where(rest == m2, lane, LANES), axis=-1, keepdims=True)
    e2 = jnp.exp(m2 - m1)
    denom = 1.0 + e2
    rw_ref[...] = jnp.where(lane == 0, 1.0 / denom, jnp.where(lane == 1, e2 / denom, 0.0))

    sel1 = lane == i1
    sel2 = lane == i2
    sel = jnp.where(sel1 | sel2, 1.0, 0.0)
    t_row = lax.broadcasted_iota(I32, (TM_PROJ, TM_PROJ), 0)
    t_col = lax.broadcasted_iota(I32, (TM_PROJ, TM_PROJ), 1)
    earlier = jnp.where(t_col < t_row, 1.0, 0.0).astype(BF16)
    rank = run_ref[...] + jnp.dot(earlier, sel.astype(BF16), preferred_element_type=F32)
    rank1 = jnp.sum(jnp.where(sel1, rank, 0.0), axis=-1, keepdims=True).astype(I32)
    rank2 = jnp.sum(jnp.where(sel2, rank, 0.0), axis=-1, keepdims=True).astype(I32)
    ri = jnp.where(lane == 0, i1, jnp.where(lane == 1, i2,
                   jnp.where(lane == 2, rank1, jnp.where(lane == 3, rank2, 0))))
    ri_ref[...] = ri.T[:8, :]
    run = run_ref[...] + jnp.sum(sel, axis=0, keepdims=True)
    run_ref[...] = run
    cnt_ref[...] = jnp.broadcast_to(run, cnt_ref.shape)


def _proj_route(o, h, w_out, gain, w_router_hi, w_router_lo):
    n = h.shape[0]
    tok = lambda i: (i, 0)
    return pl.pallas_call(
        _proj_route_kernel,
        grid=(n // TM_PROJ,),
        in_specs=[
            pl.BlockSpec((TM_PROJ, D_MODEL), tok),
            pl.BlockSpec((TM_PROJ, D_MODEL), tok),
            _const_spec((D_MODEL, D_MODEL)),
            _const_spec((1, D_MODEL)),
            _const_spec((D_MODEL, LANES)),
            _const_spec((D_MODEL, LANES)),
        ],
        out_specs=[
            pl.BlockSpec((TM_PROJ, D_MODEL), tok),
            pl.BlockSpec((TM_PROJ * SLAB, LANES), tok),
            pl.BlockSpec((8, TM_PROJ), lambda i: (0, i)),
            pl.BlockSpec((TM_PROJ, LANES), tok),
            pl.BlockSpec((8, LANES), lambda i: (0, 0)),
        ],
        out_shape=[
            jax.ShapeDtypeStruct((n, D_MODEL), F32),
            jax.ShapeDtypeStruct((n * SLAB, LANES), F32),
            jax.ShapeDtypeStruct((8, n), I32),
            jax.ShapeDtypeStruct((n, LANES), F32),
            jax.ShapeDtypeStruct((8, LANES), F32),
        ],
        scratch_shapes=[pltpu.VMEM((1, LANES), F32)],
        compiler_params=_params("arbitrary"),
        name="proj_route",
    )(o, h, w_out, gain, w_router_hi, w_router_lo)


def _row_copy(src_ref, src_row, dst_ref, dst_row, sem):
    return pltpu.make_async_copy(src_ref.at[_slab_rows(src_row)], dst_ref.at[_slab_rows(dst_row)], sem)


def _dispatch_kernel(d0_ref, d1_ref, zrow_ref, hn_ref, xs_ref, zero_ref, zsem, sem):
    i = pl.program_id(0)

    def zero_fill(z):
        rows = pl.ds(pl.multiple_of(zrow_ref[z] * SLAB, SLAB), TM_GROUP * SLAB)
        return pltpu.make_async_copy(zero_ref, xs_ref.at[rows], zsem)

    @pl.when(i == 0)
    def _():
        zero_ref[...] = jnp.zeros(zero_ref.shape, F32)
        for z in range(2 * N_EXPERTS):
            @pl.when(zrow_ref[z] >= 0)
            def _():
                zero_fill(z).start()
        for z in range(2 * N_EXPERTS):
            @pl.when(zrow_ref[z] >= 0)
            def _():
                zero_fill(z).wait()

    base = i * TM_MOVE

    def issue(t, carry):
        _row_copy(hn_ref, t, xs_ref, d0_ref[base + t], sem).start(priority=0)
        _row_copy(hn_ref, t, xs_ref, d1_ref[base + t], sem).start(priority=1)
        return carry

    lax.fori_loop(0, TM_MOVE, issue, 0, unroll=8)

    def drain(t, carry):
        _row_copy(hn_ref, t, xs_ref, d0_ref[base + t], sem).wait()
        _row_copy(hn_ref, t, xs_ref, d1_ref[base + t], sem).wait()
        return carry

    lax.fori_loop(0, TM_MOVE, drain, 0, unroll=8)


def _dispatch(d0, d1, zero_rows, hn, rows):
    n = hn.shape[0] // SLAB
    return pl.pallas_call(
        _dispatch_kernel,
        grid_spec=pltpu.PrefetchScalarGridSpec(
            num_scalar_prefetch=3,
            grid=(n // TM_MOVE,),
            in_specs=[pl.BlockSpec((TM_MOVE * SLAB, LANES), lambda i, d0, d1, z: (i, 0))],
            out_specs=pl.BlockSpec(memory_space=pl.ANY),
            scratch_shapes=[
                pltpu.VMEM((TM_GROUP * SLAB, LANES), F32),
                pltpu.SemaphoreType.DMA,
                pltpu.SemaphoreType.DMA,
            ],
        ),
        out_shape=jax.ShapeDtypeStruct((rows * SLAB, LANES), F32),
        compiler_params=_params("arbitrary"),
        name="moe_dispatch",
    )(d0, d1, zero_rows, hn)


def _group_ffn_kernel(te_ref, tv_ref, x_ref, wg_ref, wu_ref, wd_ref, o_ref, xb_ref, acc_ref):
    g = pl.program_id(0)
    j = pl.program_id(1)

    @pl.when((tv_ref[g] == 0) & (j == 0))
    def _():
        o_ref[...] = jnp.zeros(o_ref.shape, F32)

    @pl.when(tv_ref[g] > 0)
    def _():
        @pl.when(j == 0)
        def _():
            for s in range(SLAB):
                xb_ref[:, s * LANES:(s + 1) * LANES] = _load_slab_cols(
                    x_ref, s, TM_GROUP).astype(BF16)
            acc_ref[...] = jnp.zeros(acc_ref.shape, F32)

        x = xb_ref[...]
        gate = jnp.dot(x, wg_ref[0], preferred_element_type=F32)
        up = jnp.dot(x, wu_ref[0], preferred_element_type=F32)
        act = (jax.nn.silu(gate) * up).astype(BF16)
        acc_ref[...] += jnp.dot(act, wd_ref[0], preferred_element_type=F32)

        @pl.when(j == pl.num_programs(1) - 1)
        def _():
            _store_slabs(o_ref, acc_ref[...])


def _group_ffn(tile_expert, tile_valid, xs, w_gate, w_up, w_down):
    rows = xs.shape[0] // SLAB
    n_chunks = FFN_EXPERT // FFN_EXPERT_CHUNK
    last = n_chunks - 1
    chunk = lambda g, j, tv: jnp.where(g % 2 == 0, j, last - j)
    row_map = lambda g, j, te, tv: (g, 0)
    return pl.pallas_call(
        _group_ffn_kernel,
        grid_spec=pltpu.PrefetchScalarGridSpec(
            num_scalar_prefetch=2,
            grid=(rows // TM_GROUP, n_chunks),
            in_specs=[
                pl.BlockSpec((TM_GROUP * SLAB, LANES), row_map),
                pl.BlockSpec((1, D_MODEL, FFN_EXPERT_CHUNK),
                             lambda g, j, te, tv: (te[g], 0, chunk(g, j, tv))),
                pl.BlockSpec((1, D_MODEL, FFN_EXPERT_CHUNK),
                             lambda g, j, te, tv: (te[g], 0, chunk(g, j, tv))),
                pl.BlockSpec((1, FFN_EXPERT_CHUNK, D_MODEL),
                             lambda g, j, te, tv: (te[g], chunk(g, j, tv), 0)),
            ],
            out_specs=pl.BlockSpec((TM_GROUP * SLAB, LANES), row_map),
            scratch_shapes=[
                pltpu.VMEM((TM_GROUP, D_MODEL), BF16),
                pltpu.VMEM((TM_GROUP, D_MODEL), F32),
            ],
        ),
        out_shape=jax.ShapeDtypeStruct((rows * SLAB, LANES), F32),
        compiler_params=_params("arbitrary", "arbitrary"),
        name="moe_group_ffn",
    )(tile_expert, tile_valid, xs, w_gate, w_up, w_down)


def _combine_kernel(d0_ref, d1_ref, h_ref, rw_ref, ys_ref, o_ref, r0_ref, r1_ref, sems):
    i = pl.program_id(0)
    slot = i % 2

    def gather(tile, buf, action):
        base = tile * TM_MOVE
        r0, r1, sem = r0_ref.at[buf], r1_ref.at[buf], sems.at[buf]

        def body(t, carry):
            c0 = _row_copy(ys_ref, d0_ref[base + t], r0, t, sem)
            c1 = _row_copy(ys_ref, d1_ref[base + t], r1, t, sem)
            if action == "start":
                c0.start(priority=0)
                c1.start(priority=1)
            else:
                c0.wait()
                c1.wait()
            return carry

        lax.fori_loop(0, TM_MOVE, body, 0, unroll=8)

    @pl.when(i == 0)
    def _():
        gather(0, 0, "start")

    @pl.when(i + 1 < pl.num_programs(0))
    def _():
        gather(i + 1, 1 - slot, "start")

    gather(i, slot, "wait")
    rw = rw_ref[...]
    w1 = rw[:, 0:1]
    w2 = rw[:, 1:2]
    r0, r1 = r0_ref.at[slot], r1_ref.at[slot]
    for s in range(SLAB):
        cols = slice(s * LANES, (s + 1) * LANES)
        o_ref[:, cols] = (h_ref[:, cols] + w1 * _load_slab_cols(r0, s, TM_MOVE)
                          + w2 * _load_slab_cols(r1, s, TM_MOVE))


def _combine(d0, d1, h, route_w, ys):
    n = h.shape[0]
    tok = lambda i, d0, d1: (i, 0)
    return pl.pallas_call(
        _combine_kernel,
        grid_spec=pltpu.PrefetchScalarGridSpec(
            num_scalar_prefetch=2,
            grid=(n // TM_MOVE,),
            in_specs=[
                pl.BlockSpec((TM_MOVE, D_MODEL), tok),
                pl.BlockSpec((TM_MOVE, LANES), tok),
                pl.BlockSpec(memory_space=pl.ANY),
            ],
            out_specs=pl.BlockSpec((TM_MOVE, D_MODEL), tok),
            scratch_shapes=[
                pltpu.VMEM((2, TM_MOVE * SLAB, LANES), F32),
                pltpu.VMEM((2, TM_MOVE * SLAB, LANES), F32),
                pltpu.SemaphoreType.DMA((2,)),
            ],
        ),
        out_shape=jax.ShapeDtypeStruct((n, D_MODEL), F32),
        compiler_params=_params("arbitrary"),
        name="moe_combine",
    )(d0, d1, h, route_w, ys)


def _routing_tables(route_i, counts, n):
    counts = counts.astype(I32)
    padded = (counts + TM_GROUP - 1) // TM_GROUP * TM_GROUP
    ends = jnp.cumsum(padded)
    starts = ends - padded
    d0 = starts[route_i[0]] + route_i[2]
    d1 = starts[route_i[1]] + route_i[3]
    n_tiles = TOP_K * n // TM_GROUP + N_EXPERTS
    used = ends[-1] // TM_GROUP
    g = jnp.arange(n_tiles, dtype=I32)
    first_row = jnp.minimum(g, used - 1) * TM_GROUP
    tile_expert = jnp.sum((ends[None, :] <= first_row[:, None]).astype(I32), axis=1)
    tile_valid = (g < used).astype(I32)
    tail = g[-N_EXPERTS:]
    zero_rows = jnp.concatenate([
        jnp.where(padded > 0, ends - TM_GROUP, -1),
        jnp.where(tail >= used, tail * TM_GROUP, -1)]).astype(I32)
    return d0, d1, zero_rows, tile_expert, tile_valid, n_tiles * TM_GROUP


def kernel(x, positions, l0_mix_norm, l0_sg_w_in, l0_sg_v_norm, l0_sg_w_spatial, l0_sg_b_spatial, l0_sg_w_out, l0_ffn_norm, l0_ffn_w_gate, l0_ffn_w_up, l0_ffn_w_down, l1_mix_norm, l1_da_w_qkv, l1_da_q_norm, l1_da_k_norm, l1_da_lambda_q1, l1_da_lambda_k1, l1_da_lambda_q2, l1_da_lambda_k2, l1_da_subln, l1_da_w_out, l1_moe_norm, l1_moe_w_router, l1_moe_w_gate, l1_moe_w_up, l1_moe_w_down):
    batch, seq, _ = x.shape
    n = batch * seq
    row = lambda a: a.reshape(1, -1)
    h = x.reshape(n, D_MODEL)

    h = _sg_mixer(h, row(l0_mix_norm), l0_sg_w_in.astype(BF16), row(l0_sg_v_norm),
                  l0_sg_w_spatial, l0_sg_b_spatial.T, l0_sg_w_out.astype(BF16))
    h = _dense_ffn(h, row(l0_ffn_norm), l0_ffn_w_gate.astype(BF16),
                   l0_ffn_w_up.astype(BF16), l0_ffn_w_down.astype(BF16))

    inv_freq = 1.0 / (ROPE_THETA ** (jnp.arange(0, ROT_DIM, 2, dtype=F32) / ROT_DIM))
    d = jnp.arange(HEAD_WIDTH) % DA_HEAD_DIM
    freq_lane = jnp.where(d < ROT_DIM, inv_freq[d % ROT_HALF], 0.0).reshape(1, HEAD_WIDTH)
    seg = jnp.arange(HEAD_WIDTH) // DA_HEAD_DIM
    seg_ones = (seg[:, None] == seg[None, :]).astype(BF16)
    q, k, vt = _qkv(h, positions.reshape(n, 1), row(l1_mix_norm), l1_da_w_qkv.astype(BF16),
                    freq_lane, row(jnp.tile(l1_da_q_norm, 2)), row(jnp.tile(l1_da_k_norm, 2)),
                    seg_ones, batch, seq)
    lam_terms = jnp.stack([l1_da_lambda_q1, l1_da_lambda_k1, l1_da_lambda_q2, l1_da_lambda_k2])
    o = _attention(q, k, vt, lam_terms, row(l1_da_subln), batch, seq)

    w_router = jnp.pad(l1_moe_w_router, ((0, 0), (0, LANES - N_EXPERTS)))
    w_router_hi = w_router.astype(BF16)
    w_router_lo = (w_router - w_router_hi.astype(F32)).astype(BF16)
    h, hn, route_i, route_w, counts = _proj_route(
        o, h, l1_da_w_out.astype(BF16), row(l1_moe_norm), w_router_hi, w_router_lo)
    d0, d1, zero_rows, tile_expert, tile_valid, rows = _routing_tables(
        route_i, counts[0, :N_EXPERTS], n)
    xs = _dispatch(d0, d1, zero_rows, hn, rows)
    ys = _group_ffn(tile_expert, tile_valid, xs, l1_moe_w_gate.astype(BF16),
                    l1_moe_w_up.astype(BF16), l1_moe_w_down.astype(BF16))
    h = _combine(d0, d1, h, route_w, ys)
    return h.reshape(batch, seq, D_MODEL)
```

```python
import math

import jax
import jax.numpy as jnp
from jax import lax
from jax.experimental import pallas as pl
from jax.experimental.pallas import tpu as pltpu

F32 = jnp.float32
BF16 = jnp.bfloat16
I32 = jnp.int32

D_MODEL = 1024
SG_WIDTH = 2 * D_MODEL
SG_GROUPS = 8
SG_CHUNK = 128
SG_GROUP_WIDTH = SG_WIDTH // SG_GROUPS
DA_HEADS = 8
DA_HEAD_DIM = 64
HEAD_WIDTH = 2 * DA_HEAD_DIM
ROT_DIM = DA_HEAD_DIM // 4
ROT_HALF = ROT_DIM // 2
ROPE_THETA = 500000.0
FFN_DENSE = 2816
N_EXPERTS = 8
TOP_K = 2
FFN_EXPERT = 3584
NORM_EPS = 1e-6
LAMBDA_INIT = 0.8 - 0.6 * math.exp(-0.3 * 1)

LANES = 128
VMEM_LIMIT = 56 * 1024 * 1024

TM_SG = 512
TM_FFN = 512
TM_QKV = 512
TK = 1024
TQ = 1024
QC = 256
VT_ROWS = HEAD_WIDTH + 16
TM_PROJ = 512
TM_GROUP = 512
FFN_EXPERT_CHUNK = 1792
TM_MOVE = 512
SLAB = D_MODEL // LANES
assert SLAB == 8


def _slab_rows(t):
    return pl.ds(pl.multiple_of(t * SLAB, SLAB), SLAB)


def _load_slab_cols(ref, s, tokens):
    return ref[pl.ds(s, tokens, stride=SLAB), :]


def _store_slabs(ref, x):
    for s in range(SLAB):
        ref[pl.ds(s, x.shape[0], stride=SLAB), :] = x[:, s * LANES:(s + 1) * LANES]


def _rms(x, gain):
    return x * lax.rsqrt(jnp.mean(x * x, axis=-1, keepdims=True) + NORM_EPS) * gain


def _const_spec(shape):
    zeros = (0,) * len(shape)
    return pl.BlockSpec(shape, lambda *_: zeros, pipeline_mode=pl.Buffered(1))


def _params(*semantics):
    return pltpu.CompilerParams(dimension_semantics=semantics, vmem_limit_bytes=VMEM_LIMIT)


def _sg_mixer_kernel(x_ref, g_ref, win_ref, vg_ref, wsp_ref, bt_ref, wout_ref, o_ref, y_ref):
    x = x_ref[...]
    hn = _rms(x, g_ref[...]).astype(BF16)
    z = jax.nn.gelu(jnp.dot(hn, win_ref[...], preferred_element_type=F32))
    u = z[:, :SG_WIDTH]
    v = _rms(z[:, SG_WIDTH:], vg_ref[...]).astype(BF16)
    row = lax.broadcasted_iota(I32, (SG_CHUNK, SG_CHUNK), 0)
    col = lax.broadcasted_iota(I32, (SG_CHUNK, SG_CHUNK), 1)
    causal = col <= row
    for g in range(SG_GROUPS):
        w_masked = jnp.where(causal, wsp_ref[g], 0.0).astype(BF16)
        bias = bt_ref[:, g:g + 1]
        cols = slice(g * SG_GROUP_WIDTH, (g + 1) * SG_GROUP_WIDTH)
        for c in range(TM_SG // SG_CHUNK):
            rows = slice(c * SG_CHUNK, (c + 1) * SG_CHUNK)
            mixed = jnp.dot(w_masked, v[rows, cols], preferred_element_type=F32) + bias
            y_ref[rows, cols] = (u[rows, cols] * mixed).astype(BF16)
    o_ref[...] = x + jnp.dot(y_ref[...], wout_ref[...], preferred_element_type=F32)


def _sg_mixer(x, gain, w_in, v_gain, w_spatial, b_spatial_t, w_out):
    n = x.shape[0]
    tok = lambda i: (i, 0)
    return pl.pallas_call(
        _sg_mixer_kernel,
        grid=(n // TM_SG,),
        in_specs=[
            pl.BlockSpec((TM_SG, D_MODEL), tok),
            _const_spec((1, D_MODEL)),
            _const_spec((D_MODEL, 2 * SG_WIDTH)),
            _const_spec((1, SG_WIDTH)),
            _const_spec((SG_GROUPS, SG_CHUNK, SG_CHUNK)),
            _const_spec((SG_CHUNK, SG_GROUPS)),
            _const_spec((SG_WIDTH, D_MODEL)),
        ],
        out_specs=pl.BlockSpec((TM_SG, D_MODEL), tok),
        out_shape=jax.ShapeDtypeStruct((n, D_MODEL), F32),
        scratch_shapes=[pltpu.VMEM((TM_SG, SG_WIDTH), BF16)],
        compiler_params=_params("parallel"),
        name="sg_mixer",
    )(x, gain, w_in, v_gain, w_spatial, b_spatial_t, w_out)


def _dense_ffn_kernel(x_ref, g_ref, wg_ref, wu_ref, wd_ref, o_ref):
    x = x_ref[...]
    hn = _rms(x, g_ref[...]).astype(BF16)
    gate = jnp.dot(hn, wg_ref[...], preferred_element_type=F32)
    up = jnp.dot(hn, wu_ref[...], preferred_element_type=F32)
    act = (jax.nn.silu(gate) * up).astype(BF16)
    o_ref[...] = x + jnp.dot(act, wd_ref[...], preferred_element_type=F32)


def _dense_ffn(x, gain, w_gate, w_up, w_down):
    n = x.shape[0]
    tok = lambda i: (i, 0)
    return pl.pallas_call(
        _dense_ffn_kernel,
        grid=(n // TM_FFN,),
        in_specs=[
            pl.BlockSpec((TM_FFN, D_MODEL), tok),
            _const_spec((1, D_MODEL)),
            _const_spec((D_MODEL, FFN_DENSE)),
            _const_spec((D_MODEL, FFN_DENSE)),
            _const_spec((FFN_DENSE, D_MODEL)),
        ],
        out_specs=pl.BlockSpec((TM_FFN, D_MODEL), tok),
        out_shape=jax.ShapeDtypeStruct((n, D_MODEL), F32),
        compiler_params=_params("parallel"),
        name="dense_ffn",
    )(x, gain, w_gate, w_up, w_down)


def _qkv_kernel(x_ref, pos_ref, g_ref, w_ref, freq_ref, qg_ref, kg_ref, seg_ref,
                q_ref, k_ref, vt_ref):
    hn = _rms(x_ref[...], g_ref[...]).astype(BF16)
    qkv = jnp.dot(hn, w_ref[...], preferred_element_type=F32)

    ang = pos_ref[...].astype(F32) * freq_ref[...]
    cos = jnp.cos(ang)
    sin = jnp.sin(ang)
    d = lax.broadcasted_iota(I32, (1, HEAD_WIDTH), 1) % DA_HEAD_DIM
    sin_lo = jnp.where(d < ROT_HALF, -sin, 0.0)
    sin_hi = jnp.where((d >= ROT_HALF) & (d < ROT_DIM), sin, 0.0)
    seg = seg_ref[...]

    def norm_rope(xh, gain):
        ssq = jnp.dot((xh * xh).astype(BF16), seg, preferred_element_type=F32)
        xn = xh * lax.rsqrt(ssq * (1.0 / DA_HEAD_DIM) + NORM_EPS) * gain
        nxt = pltpu.roll(xn, HEAD_WIDTH - ROT_HALF, 1)
        prv = pltpu.roll(xn, ROT_HALF, 1)
        return xn * cos + nxt * sin_lo + prv * sin_hi

    scale = math.log2(math.e) / math.sqrt(DA_HEAD_DIM)
    for h in range(DA_HEADS):
        cols = slice(h * HEAD_WIDTH, (h + 1) * HEAD_WIDTH)
        q = norm_rope(qkv[:, cols], qg_ref[...])
        q_ref[:, cols] = (q * scale).astype(BF16)
        kcols = slice(D_MODEL + h * HEAD_WIDTH, D_MODEL + (h + 1) * HEAD_WIDTH)
        k_ref[:, cols] = norm_rope(qkv[:, kcols], kg_ref[...]).astype(BF16)
        vcols = slice(2 * D_MODEL + h * HEAD_WIDTH, 2 * D_MODEL + (h + 1) * HEAD_WIDTH)
        vt_ref[0, h, 0, :HEAD_WIDTH, :] = qkv[:, vcols].T.astype(BF16)
        vt_ref[0, h, 0, HEAD_WIDTH:, :] = jnp.ones((VT_ROWS - HEAD_WIDTH, TM_QKV), BF16)


def _qkv(x, pos, gain, w_qkv, freq_lane, q_gain, k_gain, seg_ones, batch, seq):
    n = x.shape[0]
    tiles = seq // TM_QKV
    per_key_tile = TK // TM_QKV
    tok = lambda i: (i, 0)
    out = jax.ShapeDtypeStruct((n, D_MODEL), BF16)
    return pl.pallas_call(
        _qkv_kernel,
        grid=(n // TM_QKV,),
        in_specs=[
            pl.BlockSpec((TM_QKV, D_MODEL), tok),
            pl.BlockSpec((TM_QKV, 1), tok),
            _const_spec((1, D_MODEL)),
            _const_spec((D_MODEL, 3 * D_MODEL)),
            _const_spec((1, HEAD_WIDTH)),
            _const_spec((1, HEAD_WIDTH)),
            _const_spec((1, HEAD_WIDTH)),
            _const_spec((HEAD_WIDTH, HEAD_WIDTH)),
        ],
        out_specs=[
            pl.BlockSpec((TM_QKV, D_MODEL), tok),
            pl.BlockSpec((TM_QKV, D_MODEL), tok),
            pl.BlockSpec((1, DA_HEADS, 1, VT_ROWS, TM_QKV),
                         lambda i: (i // tiles, 0, (i % tiles) // per_key_tile, 0,
                                    i % per_key_tile)),
        ],
        out_shape=[out, out,
                   jax.ShapeDtypeStruct((batch, DA_HEADS, seq // TK, VT_ROWS, TK), BF16)],
        compiler_params=_params("parallel"),
        name="qkv_norm_rope",
    )(x, pos, gain, w_qkv, freq_lane, q_gain, k_gain, seg_ones)


def _attn_kernel(lam_ref, sub_ref, q_ref, k_ref, vt_ref, wg_ref, wu_ref, wd_ref,
                 o_ref, wg_bf_ref, wu_bf_ref, wd_bf_ref,
                 qqt_ref, m_ref, acc_ref, sc_ref, pc_ref, ac_ref):
    for src, dst in ((wg_ref, wg_bf_ref), (wu_ref, wu_bf_ref), (wd_ref, wd_bf_ref)):
        dst[...] = src[...].astype(BF16)
    qi = pl.program_id(2)
    qt = q_ref[0].astype(F32).T
    first = lax.broadcasted_iota(I32, (HEAD_WIDTH, TQ), 0) < DA_HEAD_DIM
    qqt_ref[:, :TQ] = jnp.where(first, qt, 0.0).astype(BF16)
    qqt_ref[:, TQ:] = jnp.where(first, 0.0, qt).astype(BF16)
    m_ref[...] = jnp.full(m_ref.shape, -jnp.inf, F32)
    acc_ref[...] = jnp.zeros(acc_ref.shape, F32)

    n_chunks = 2 * TQ // QC
    chunk_cols = [slice(c * QC, (c + 1) * QC) for c in range(n_chunks)]

    def scores(k, cols):
        return jnp.dot(k, qqt_ref[:, cols], preferred_element_type=F32)

    visible = (lax.broadcasted_iota(I32, (QC, QC), 0)
               <= lax.broadcasted_iota(I32, (QC, QC), 1))

    def softmax(s, cols, q_offset):
        if q_offset is not None:
            tail = jnp.where(visible, s[q_offset:], -jnp.inf)
            s = jnp.concatenate([s[:q_offset], tail], axis=0) if q_offset else tail
        m_old = m_ref[:, cols]
        m_new = jnp.maximum(m_old, jnp.max(s, axis=0, keepdims=True))
        alpha = jnp.exp2(m_old - m_new)
        p = jnp.exp2((s - m_new).astype(BF16))
        m_ref[:, cols] = m_new
        return p, alpha

    def accumulate(p, alpha, vt, cols):
        acc_ref[:, cols] = alpha * acc_ref[:, cols] + jnp.dot(
            vt, p, preferred_element_type=F32)

    def k_block(j):
        return k_ref[0, pl.ds(pl.multiple_of(j * TK, TK), TK), :]

    def full_block(j, carry):
        s_cur, p_prev, alpha_prev = sc_ref[...], pc_ref[...], ac_ref[...]
        k = k_block(j)
        vt = vt_ref[0, 0, j]
        for c in range(n_chunks):
            if c + 1 < n_chunks:
                s_next = scores(k, chunk_cols[c + 1])
            else:
                sc_ref[...] = scores(k_block(j + 1), chunk_cols[0])
            p, alpha = softmax(s_cur, chunk_cols[c], None)
            vt_prev = vt_ref[0, 0, jnp.maximum(j - 1, 0)] if c == 0 else vt
            accumulate(p_prev, alpha_prev, vt_prev, chunk_cols[c - 1])
            s_cur, p_prev, alpha_prev = s_next, p, alpha
        pc_ref[...] = p_prev
        ac_ref[...] = alpha_prev
        return carry

    sc_ref[...] = scores(k_block(0), chunk_cols[0])
    pc_ref[...] = jnp.zeros(pc_ref.shape, BF16)
    ac_ref[...] = jnp.ones(ac_ref.shape, F32)
    lax.fori_loop(0, qi, full_block, 0)

    k_diag = k_block(qi)
    vt_diag = vt_ref[0, 0, qi]
    q_offsets = [(c * QC) % TQ for c in range(n_chunks)]
    nkeys = [q0 + QC for q0 in q_offsets]
    s_cur, p_prev, alpha_prev = sc_ref[:nkeys[0], :], pc_ref[...], ac_ref[...]
    vt_prev = vt_ref[0, 0, jnp.maximum(qi - 1, 0)]
    for c in range(n_chunks):
        if c + 1 < n_chunks:
            s_next = scores(k_diag[:nkeys[c + 1]], chunk_cols[c + 1])
        p, alpha = softmax(s_cur, chunk_cols[c], q_offsets[c])
        accumulate(p_prev, alpha_prev, vt_prev, chunk_cols[c - 1])
        s_cur, p_prev, alpha_prev, vt_prev = s_next, p, alpha, vt_diag[:, :nkeys[c]]
    accumulate(p_prev, alpha_prev, vt_prev, chunk_cols[n_chunks - 1])

    lam_terms = lam_ref[...]
    lam = (jnp.exp(jnp.sum(lam_terms[0:1] * lam_terms[1:2]))
           - jnp.exp(jnp.sum(lam_terms[2:3] * lam_terms[3:4])) + LAMBDA_INIT)
    pv = acc_ref[:HEAD_WIDTH, :]
    l = acc_ref[HEAD_WIDTH:HEAD_WIDTH + 1, :]
    ot = pv[:, :TQ] / l[:, :TQ] - lam * (pv[:, TQ:] / l[:, TQ:])
    o_ref[0] = (_rms(ot.T, sub_ref[...]) * (1.0 - LAMBDA_INIT)).astype(o_ref.dtype)


def _attention(q, k, vt, lam_terms, subln, batch, seq, expert_weights):
    q = q.reshape(batch, seq, D_MODEL)
    k = k.reshape(batch, seq, D_MODEL)
    q_tiles = seq // TQ
    steps = batch * DA_HEADS * q_tiles
    q_spec = pl.BlockSpec((1, TQ, HEAD_WIDTH), lambda b, h, i: (b, i, h))
    step_rows = lambda b, h, i: ((b * DA_HEADS + h) * q_tiles + i, 0)
    flat = [w.reshape(-1, w.shape[-1]) for w in expert_weights]
    cast_specs = [pl.BlockSpec((w.shape[0] // steps, w.shape[1]), step_rows) for w in flat]
    assert all(w.shape[0] % (16 * steps) == 0 for w in flat)
    o, *cast = pl.pallas_call(
        _attn_kernel,
        grid=(batch, DA_HEADS, q_tiles),
        in_specs=[
            _const_spec((4, DA_HEAD_DIM)),
            _const_spec((1, HEAD_WIDTH)),
            q_spec,
            pl.BlockSpec((1, seq, HEAD_WIDTH), lambda b, h, i: (b, 0, h)),
            pl.BlockSpec((1, 1, seq // TK, VT_ROWS, TK), lambda b, h, i: (b, h, 0, 0, 0)),
        ] + cast_specs,
        out_specs=[q_spec] + cast_specs,
        out_shape=[jax.ShapeDtypeStruct((batch, seq, D_MODEL), BF16)]
        + [jax.ShapeDtypeStruct(w.shape, BF16) for w in flat],
        scratch_shapes=[
            pltpu.VMEM((HEAD_WIDTH, 2 * TQ), BF16),
            pltpu.VMEM((1, 2 * TQ), F32),
            pltpu.VMEM((VT_ROWS, 2 * TQ), F32),
            pltpu.VMEM((TK, QC), F32),
            pltpu.VMEM((TK, QC), BF16),
            pltpu.VMEM((1, QC), F32),
        ],
        compiler_params=_params("parallel", "parallel", "arbitrary"),
        name="diff_attention",
    )(lam_terms, subln, q, k, vt, *flat)
    cast = [c.reshape(w.shape) for c, w in zip(cast, expert_weights)]
    return o.reshape(batch * seq, D_MODEL), cast


def _proj_route_kernel(o_ref, h_ref, wo_ref, g_ref, wrh_ref, wrl_ref,
                       h2_ref, hn_ref, ri_ref, rw_ref, cnt_ref, run_ref):
    @pl.when(pl.program_id(0) == 0)
    def _():
        run_ref[...] = jnp.zeros(run_ref.shape, F32)

    h2 = h_ref[...] + jnp.dot(o_ref[...], wo_ref[...], preferred_element_type=F32)
    h2_ref[...] = h2
    hn = _rms(h2, g_ref[...])
    _store_slabs(hn_ref, hn)
    hn_hi = hn.astype(BF16)
    hn_lo = (hn - hn_hi.astype(F32)).astype(BF16)
    logits = (jnp.dot(hn_hi, wrh_ref[...], preferred_element_type=F32)
              + jnp.dot(hn_lo, wrh_ref[...], preferred_element_type=F32)
              + jnp.dot(hn_hi, wrl_ref[...], preferred_element_type=F32))
    lane = lax.broadcasted_iota(I32, logits.shape, 1)
    logits = jnp.where(lane < N_EXPERTS, logits, -jnp.inf)
    m1 = jnp.max(logits, axis=-1, keepdims=True)
    i1 = jnp.min(jnp.where(logits == m1, lane, LANES), axis=-1, keepdims=True)
    rest = jnp.where(lane == i1, -jnp.inf, logits)
    m2 = jnp.max(rest, axis=-1, keepdims=True)
    i2 = jnp.min(jnp.where(rest == m2, lane, LANES), axis=-1, keepdims=True)
    e2 = jnp.exp(m2 - m1)
    denom = 1.0 + e2
    rw_ref[...] = jnp.where(lane == 0, 1.0 / denom, jnp.where(lane == 1, e2 / denom, 0.0))

    sel1 = lane == i1
    sel2 = lane == i2
    sel = jnp.where(sel1 | sel2, 1.0, 0.0)
    t_row = lax.broadcasted_iota(I32, (TM_PROJ, TM_PROJ), 0)
    t_col = lax.broadcasted_iota(I32, (TM_PROJ, TM_PROJ), 1)
    earlier = jnp.where(t_col < t_row, 1.0, 0.0).astype(BF16)
    rank = run_ref[...] + jnp.dot(earlier, sel.astype(BF16), preferred_element_type=F32)
    rank1 = jnp.sum(jnp.where(sel1, rank, 0.0), axis=-1, keepdims=True).astype(I32)
    rank2 = jnp.sum(jnp.where(sel2, rank, 0.0), axis=-1, keepdims=True).astype(I32)
    ri = jnp.where(lane == 0, i1, jnp.where(lane == 1, i2,
                   jnp.where(lane == 2, rank1, jnp.where(lane == 3, rank2, 0))))
    ri_ref[...] = ri.T[:8, :]
    run = run_ref[...] + jnp.sum(sel, axis=0, keepdims=True)
    run_ref[...] = run
    cnt_ref[...] = jnp.broadcast_to(run, cnt_ref.shape)


def _proj_route(o, h, w_out, gain, w_router_hi, w_router_lo):
    n = h.shape[0]
    tok = lambda i: (i, 0)
    return pl.pallas_call(
        _proj_route_kernel,
        grid=(n // TM_PROJ,),
        in_specs=[
            pl.BlockSpec((TM_PROJ, D_MODEL), tok),
            pl.BlockSpec((TM_PROJ, D_MODEL), tok),
            _const_spec((D_MODEL, D_MODEL)),
            _const_spec((1, D_MODEL)),
            _const_spec((D_MODEL, LANES)),
            _const_spec((D_MODEL, LANES)),
        ],
        out_specs=[
            pl.BlockSpec((TM_PROJ, D_MODEL), tok),
            pl.BlockSpec((TM_PROJ * SLAB, LANES), tok),
            pl.BlockSpec((8, TM_PROJ), lambda i: (0, i)),
            pl.BlockSpec((TM_PROJ, LANES), tok),
            pl.BlockSpec((8, LANES), lambda i: (0, 0)),
        ],
        out_shape=[
            jax.ShapeDtypeStruct((n, D_MODEL), F32),
            jax.ShapeDtypeStruct((n * SLAB, LANES), F32),
            jax.ShapeDtypeStruct((8, n), I32),
            jax.ShapeDtypeStruct((n, LANES), F32),
            jax.ShapeDtypeStruct((8, LANES), F32),
        ],
        scratch_shapes=[pltpu.VMEM((1, LANES), F32)],
        compiler_params=_params("arbitrary"),
        name="proj_route",
    )(o, h, w_out, gain, w_router_hi, w_router_lo)


def _row_copy(src_ref, src_row, dst_ref, dst_row, sem):
    return pltpu.make_async_copy(src_ref.at[_slab_rows(src_row)], dst_ref.at[_slab_rows(dst_row)], sem)


def _dispatch_kernel(d0_ref, d1_ref, zrow_ref, hn_ref, xs_ref, zero_ref, zsem, sem):
    i = pl.program_id(0)

    def zero_fill(z):
        rows = pl.ds(pl.multiple_of(zrow_ref[z] * SLAB, SLAB), TM_GROUP * SLAB)
        return pltpu.make_async_copy(zero_ref, xs_ref.at[rows], zsem)

    @pl.when(i == 0)
    def _():
        zero_ref[...] = jnp.zeros(zero_ref.shape, F32)
        for z in range(2 * N_EXPERTS):
            @pl.when(zrow_ref[z] >= 0)
            def _():
                zero_fill(z).start()
        for z in range(2 * N_EXPERTS):
            @pl.when(zrow_ref[z] >= 0)
            def _():
                zero_fill(z).wait()

    base = i * TM_MOVE

    def issue(t, carry):
        _row_copy(hn_ref, t, xs_ref, d0_ref[base + t], sem).start(priority=0)
        _row_copy(hn_ref, t, xs_ref, d1_ref[base + t], sem).start(priority=1)
        return carry

    lax.fori_loop(0, TM_MOVE, issue, 0, unroll=8)

    def drain(t, carry):
        _row_copy(hn_ref, t, xs_ref, d0_ref[base + t], sem).wait()
        _row_copy(hn_ref, t, xs_ref, d1_ref[base + t], sem).wait()
        return carry

    lax.fori_loop(0, TM_MOVE, drain, 0, unroll=8)


def _dispatch(d0, d1, zero_rows, hn, rows):
    n = hn.shape[0] // SLAB
    return pl.pallas_call(
        _dispatch_kernel,
        grid_spec=pltpu.PrefetchScalarGridSpec(
            num_scalar_prefetch=3,
            grid=(n // TM_MOVE,),
            in_specs=[pl.BlockSpec((TM_MOVE * SLAB, LANES), lambda i, d0, d1, z: (i, 0))],
            out_specs=pl.BlockSpec(memory_space=pl.ANY),
            scratch_shapes=[
                pltpu.VMEM((TM_GROUP * SLAB, LANES), F32),
                pltpu.SemaphoreType.DMA,
                pltpu.SemaphoreType.DMA,
            ],
        ),
        out_shape=jax.ShapeDtypeStruct((rows * SLAB, LANES), F32),
        compiler_params=_params("arbitrary"),
        name="moe_dispatch",
    )(d0, d1, zero_rows, hn)


def _group_ffn_kernel(te_ref, tv_ref, x_ref, wg_ref, wu_ref, wd_ref, o_ref, xb_ref, acc_ref):
    g = pl.program_id(0)
    j = pl.program_id(1)

    @pl.when((tv_ref[g] == 0) & (j == 0))
    def _():
        o_ref[...] = jnp.zeros(o_ref.shape, F32)

    @pl.when(tv_ref[g] > 0)
    def _():
        @pl.when(j == 0)
        def _():
            for s in range(SLAB):
                xb_ref[:, s * LANES:(s + 1) * LANES] = _load_slab_cols(
                    x_ref, s, TM_GROUP).astype(BF16)
            acc_ref[...] = jnp.zeros(acc_ref.shape, F32)

        x = xb_ref[...]
        gate = jnp.dot(x, wg_ref[0], preferred_element_type=F32)
        up = jnp.dot(x, wu_ref[0], preferred_element_type=F32)
        act = (jax.nn.silu(gate) * up).astype(BF16)
        acc_ref[...] += jnp.dot(act, wd_ref[0], preferred_element_type=F32)

        @pl.when(j == pl.num_programs(1) - 1)
        def _():
            _store_slabs(o_ref, acc_ref[...])


def _group_ffn(tile_expert, tile_valid, xs, w_gate, w_up, w_down):
    rows = xs.shape[0] // SLAB
    n_chunks = FFN_EXPERT // FFN_EXPERT_CHUNK
    last = n_chunks - 1
    chunk = lambda g, j, tv: jnp.where(g % 2 == 0, j, last - j)
    row_map = lambda g, j, te, tv: (g, 0)
    return pl.pallas_call(
        _group_ffn_kernel,
        grid_spec=pltpu.PrefetchScalarGridSpec(
            num_scalar_prefetch=2,
            grid=(rows // TM_GROUP, n_chunks),
            in_specs=[
                pl.BlockSpec((TM_GROUP * SLAB, LANES), row_map),
                pl.BlockSpec((1, D_MODEL, FFN_EXPERT_CHUNK),
                             lambda g, j, te, tv: (te[g], 0, chunk(g, j, tv))),
                pl.BlockSpec((1, D_MODEL, FFN_EXPERT_CHUNK),
                             lambda g, j, te, tv: (te[g], 0, chunk(g, j, tv))),
                pl.BlockSpec((1, FFN_EXPERT_CHUNK, D_MODEL),
                             lambda g, j, te, tv: (te[g], chunk(g, j, tv), 0)),
            ],
            out_specs=pl.BlockSpec((TM_GROUP * SLAB, LANES), row_map),
            scratch_shapes=[
                pltpu.VMEM((TM_GROUP, D_MODEL), BF16),
                pltpu.VMEM((TM_GROUP, D_MODEL), F32),
            ],
        ),
        out_shape=jax.ShapeDtypeStruct((rows * SLAB, LANES), F32),
        compiler_params=_params("arbitrary", "arbitrary"),
        name="moe_group_ffn",
    )(tile_expert, tile_valid, xs, w_gate, w_up, w_down)


def _combine_kernel(d0_ref, d1_ref, h_ref, rw_ref, ys_ref, o_ref, r0_ref, r1_ref, sems):
    i = pl.program_id(0)
    slot = i % 2

    def gather(tile, buf, action):
        base = tile * TM_MOVE
        r0, r1, sem = r0_ref.at[buf], r1_ref.at[buf], sems.at[buf]

        def body(t, carry):
            c0 = _row_copy(ys_ref, d0_ref[base + t], r0, t, sem)
            c1 = _row_copy(ys_ref, d1_ref[base + t], r1, t, sem)
            if action == "start":
                c0.start(priority=0)
                c1.start(priority=1)
            else:
                c0.wait()
                c1.wait()
            return carry

        lax.fori_loop(0, TM_MOVE, body, 0, unroll=8)

    @pl.when(i == 0)
    def _():
        gather(0, 0, "start")

    @pl.when(i + 1 < pl.num_programs(0))
    def _():
        gather(i + 1, 1 - slot, "start")

    gather(i, slot, "wait")
    rw = rw_ref[...]
    w1 = rw[:, 0:1]
    w2 = rw[:, 1:2]
    r0, r1 = r0_ref.at[slot], r1_ref.at[slot]
    for s in range(SLAB):
        cols = slice(s * LANES, (s + 1) * LANES)
        o_ref[:, cols] = (h_ref[:, cols] + w1 * _load_slab_cols(r0, s, TM_MOVE)
                          + w2 * _load_slab_cols(r1, s, TM_MOVE))


def _combine(d0, d1, h, route_w, ys):
    n = h.shape[0]
    tok = lambda i, d0, d1: (i, 0)
    return pl.pallas_call(
        _combine_kernel,
        grid_spec=pltpu.PrefetchScalarGridSpec(
            num_scalar_prefetch=2,
            grid=(n // TM_MOVE,),
            in_specs=[
                pl.BlockSpec((TM_MOVE, D_MODEL), tok),
                pl.BlockSpec((TM_MOVE, LANES), tok),
                pl.BlockSpec(memory_space=pl.ANY),
            ],
            out_specs=pl.BlockSpec((TM_MOVE, D_MODEL), tok),
            scratch_shapes=[
                pltpu.VMEM((2, TM_MOVE * SLAB, LANES), F32),
                pltpu.VMEM((2, TM_MOVE * SLAB, LANES), F32),
                pltpu.SemaphoreType.DMA((2,)),
            ],
        ),
        out_shape=jax.ShapeDtypeStruct((n, D_MODEL), F32),
        compiler_params=_params("arbitrary"),
        name="moe_combine",
    )(d0, d1, h, route_w, ys)


def _routing_tables(route_i, counts, n):
    counts = counts.astype(I32)
    padded = (counts + TM_GROUP - 1) // TM_GROUP * TM_GROUP
    ends = jnp.cumsum(padded)
    starts = ends - padded
    d0 = starts[route_i[0]] + route_i[2]
    d1 = starts[route_i[1]] + route_i[3]
    n_tiles = TOP_K * n // TM_GROUP + N_EXPERTS
    used = ends[-1] // TM_GROUP
    g = jnp.arange(n_tiles, dtype=I32)
    first_row = jnp.minimum(g, used - 1) * TM_GROUP
    tile_expert = jnp.sum((ends[None, :] <= first_row[:, None]).astype(I32), axis=1)
    tile_valid = (g < used).astype(I32)
    tail = g[-N_EXPERTS:]
    zero_rows = jnp.concatenate([
        jnp.where(padded > 0, ends - TM_GROUP, -1),
        jnp.where(tail >= used, tail * TM_GROUP, -1)]).astype(I32)
    return d0, d1, zero_rows, tile_expert, tile_valid, n_tiles * TM_GROUP


def kernel(x, positions, l0_mix_norm, l0_sg_w_in, l0_sg_v_norm, l0_sg_w_spatial, l0_sg_b_spatial, l0_sg_w_out, l0_ffn_norm, l0_ffn_w_gate, l0_ffn_w_up, l0_ffn_w_down, l1_mix_norm, l1_da_w_qkv, l1_da_q_norm, l1_da_k_norm, l1_da_lambda_q1, l1_da_lambda_k1, l1_da_lambda_q2, l1_da_lambda_k2, l1_da_subln, l1_da_w_out, l1_moe_norm, l1_moe_w_router, l1_moe_w_gate, l1_moe_w_up, l1_moe_w_down):
    batch, seq, _ = x.shape
    n = batch * seq
    row = lambda a: a.reshape(1, -1)
    h = x.reshape(n, D_MODEL)

    h = _sg_mixer(h, row(l0_mix_norm), l0_sg_w_in.astype(BF16), row(l0_sg_v_norm),
                  l0_sg_w_spatial, l0_sg_b_spatial.T, l0_sg_w_out.astype(BF16))
    h = _dense_ffn(h, row(l0_ffn_norm), l0_ffn_w_gate.astype(BF16),
                   l0_ffn_w_up.astype(BF16), l0_ffn_w_down.astype(BF16))

    inv_freq = 1.0 / (ROPE_THETA ** (jnp.arange(0, ROT_DIM, 2, dtype=F32) / ROT_DIM))
    d = jnp.arange(HEAD_WIDTH) % DA_HEAD_DIM
    freq_lane = jnp.where(d < ROT_DIM, inv_freq[d % ROT_HALF], 0.0).reshape(1, HEAD_WIDTH)
    seg = jnp.arange(HEAD_WIDTH) // DA_HEAD_DIM
    seg_ones = (seg[:, None] == seg[None, :]).astype(BF16)
    q, k, vt = _qkv(h, positions.reshape(n, 1), row(l1_mix_norm), l1_da_w_qkv.astype(BF16),
                    freq_lane, row(jnp.tile(l1_da_q_norm, 2)), row(jnp.tile(l1_da_k_norm, 2)),
                    seg_ones, batch, seq)
    lam_terms = jnp.stack([l1_da_lambda_q1, l1_da_lambda_k1, l1_da_lambda_q2, l1_da_lambda_k2])
    o, (w_gate, w_up, w_down) = _attention(
        q, k, vt, lam_terms, row(l1_da_subln), batch, seq,
        (l1_moe_w_gate, l1_moe_w_up, l1_moe_w_down))

    w_router = jnp.pad(l1_moe_w_router, ((0, 0), (0, LANES - N_EXPERTS)))
    w_router_hi = w_router.astype(BF16)
    w_router_lo = (w_router - w_router_hi.astype(F32)).astype(BF16)
    h, hn, route_i, route_w, counts = _proj_route(
        o, h, l1_da_w_out.astype(BF16), row(l1_moe_norm), w_router_hi, w_router_lo)
    d0, d1, zero_rows, tile_expert, tile_valid, rows = _routing_tables(
        route_i, counts[0, :N_EXPERTS], n)
    xs = _dispatch(d0, d1, zero_rows, hn, rows)
    ys = _group_ffn(tile_expert, tile_valid, xs, w_gate, w_up, w_down)
    h = _combine(d0, d1, h, route_w, ys)
    return h.reshape(batch, seq, D_MODEL)
```

```python
import math

import jax
import jax.numpy as jnp
from jax import lax
from jax.experimental import pallas as pl
from jax.experimental.pallas import tpu as pltpu

F32 = jnp.float32
BF16 = jnp.bfloat16
I32 = jnp.int32

D_MODEL = 1024
SG_WIDTH = 2 * D_MODEL
SG_GROUPS = 8
SG_CHUNK = 128
SG_GROUP_WIDTH = SG_WIDTH // SG_GROUPS
DA_HEADS = 8
DA_HEAD_DIM = 64
HEAD_WIDTH = 2 * DA_HEAD_DIM
ROT_DIM = DA_HEAD_DIM // 4
ROT_HALF = ROT_DIM // 2
ROPE_THETA = 500000.0
FFN_DENSE = 2816
N_EXPERTS = 8
TOP_K = 2
FFN_EXPERT = 3584
NORM_EPS = 1e-6
LAMBDA_INIT = 0.8 - 0.6 * math.exp(-0.3 * 1)

LANES = 128
VMEM_LIMIT = 56 * 1024 * 1024

TM_SG = 512
TM_FFN = 512
TM_QKV = 512
TK = 1024
TQ = 1024
QC = 256
VT_ROWS = HEAD_WIDTH + 16
TM_PROJ = 512
TM_GROUP = 512
FFN_EXPERT_CHUNK = 1792
TM_MOVE = 1024
SLAB = D_MODEL // LANES
assert SLAB == 8


def _slab_rows(t):
    return pl.ds(pl.multiple_of(t * SLAB, SLAB), SLAB)


def _load_slab_cols(ref, s, tokens):
    return ref[pl.ds(s, tokens, stride=SLAB), :]


def _store_slabs(ref, x):
    for s in range(SLAB):
        ref[pl.ds(s, x.shape[0], stride=SLAB), :] = x[:, s * LANES:(s + 1) * LANES]


def _rms(x, gain):
    return x * lax.rsqrt(jnp.mean(x * x, axis=-1, keepdims=True) + NORM_EPS) * gain


def _const_spec(shape):
    zeros = (0,) * len(shape)
    return pl.BlockSpec(shape, lambda *_: zeros, pipeline_mode=pl.Buffered(1))


def _params(*semantics):
    return pltpu.CompilerParams(dimension_semantics=semantics, vmem_limit_bytes=VMEM_LIMIT)


def _sg_mixer_kernel(x_ref, g_ref, win_ref, vg_ref, wsp_ref, bt_ref, wout_ref, o_ref, y_ref):
    x = x_ref[...]
    hn = _rms(x, g_ref[...]).astype(BF16)
    z = jax.nn.gelu(jnp.dot(hn, win_ref[...], preferred_element_type=F32))
    u = z[:, :SG_WIDTH]
    v = _rms(z[:, SG_WIDTH:], vg_ref[...]).astype(BF16)
    row = lax.broadcasted_iota(I32, (SG_CHUNK, SG_CHUNK), 0)
    col = lax.broadcasted_iota(I32, (SG_CHUNK, SG_CHUNK), 1)
    causal = col <= row
    for g in range(SG_GROUPS):
        w_masked = jnp.where(causal, wsp_ref[g], 0.0).astype(BF16)
        bias = bt_ref[:, g:g + 1]
        cols = slice(g * SG_GROUP_WIDTH, (g + 1) * SG_GROUP_WIDTH)
        for c in range(TM_SG // SG_CHUNK):
            rows = slice(c * SG_CHUNK, (c + 1) * SG_CHUNK)
            mixed = jnp.dot(w_masked, v[rows, cols], preferred_element_type=F32) + bias
            y_ref[rows, cols] = (u[rows, cols] * mixed).astype(BF16)
    o_ref[...] = x + jnp.dot(y_ref[...], wout_ref[...], preferred_element_type=F32)


def _sg_mixer(x, gain, w_in, v_gain, w_spatial, b_spatial_t, w_out):
    n = x.shape[0]
    tok = lambda i: (i, 0)
    return pl.pallas_call(
        _sg_mixer_kernel,
        grid=(n // TM_SG,),
        in_specs=[
            pl.BlockSpec((TM_SG, D_MODEL), tok),
            _const_spec((1, D_MODEL)),
            _const_spec((D_MODEL, 2 * SG_WIDTH)),
            _const_spec((1, SG_WIDTH)),
            _const_spec((SG_GROUPS, SG_CHUNK, SG_CHUNK)),
            _const_spec((SG_CHUNK, SG_GROUPS)),
            _const_spec((SG_WIDTH, D_MODEL)),
        ],
        out_specs=pl.BlockSpec((TM_SG, D_MODEL), tok),
        out_shape=jax.ShapeDtypeStruct((n, D_MODEL), F32),
        scratch_shapes=[pltpu.VMEM((TM_SG, SG_WIDTH), BF16)],
        compiler_params=_params("parallel"),
        name="sg_mixer",
    )(x, gain, w_in, v_gain, w_spatial, b_spatial_t, w_out)


def _dense_ffn_kernel(x_ref, g_ref, wg_ref, wu_ref, wd_ref, o_ref):
    x = x_ref[...]
    hn = _rms(x, g_ref[...]).astype(BF16)
    gate = jnp.dot(hn, wg_ref[...], preferred_element_type=F32)
    up = jnp.dot(hn, wu_ref[...], preferred_element_type=F32)
    act = (jax.nn.silu(gate) * up).astype(BF16)
    o_ref[...] = x + jnp.dot(act, wd_ref[...], preferred_element_type=F32)


def _dense_ffn(x, gain, w_gate, w_up, w_down):
    n = x.shape[0]
    tok = lambda i: (i, 0)
    return pl.pallas_call(
        _dense_ffn_kernel,
        grid=(n // TM_FFN,),
        in_specs=[
            pl.BlockSpec((TM_FFN, D_MODEL), tok),
            _const_spec((1, D_MODEL)),
            _const_spec((D_MODEL, FFN_DENSE)),
            _const_spec((D_MODEL, FFN_DENSE)),
            _const_spec((FFN_DENSE, D_MODEL)),
        ],
        out_specs=pl.BlockSpec((TM_FFN, D_MODEL), tok),
        out_shape=jax.ShapeDtypeStruct((n, D_MODEL), F32),
        compiler_params=_params("parallel"),
        name="dense_ffn",
    )(x, gain, w_gate, w_up, w_down)


def _qkv_kernel(x_ref, pos_ref, g_ref, w_ref, freq_ref, qg_ref, kg_ref, seg_ref,
                q_ref, k_ref, vt_ref):
    hn = _rms(x_ref[...], g_ref[...]).astype(BF16)
    qkv = jnp.dot(hn, w_ref[...], preferred_element_type=F32)

    ang = pos_ref[...].astype(F32) * freq_ref[...]
    cos = jnp.cos(ang)
    sin = jnp.sin(ang)
    d = lax.broadcasted_iota(I32, (1, HEAD_WIDTH), 1) % DA_HEAD_DIM
    sin_lo = jnp.where(d < ROT_HALF, -sin, 0.0)
    sin_hi = jnp.where((d >= ROT_HALF) & (d < ROT_DIM), sin, 0.0)
    seg = seg_ref[...]

    def norm_rope(xh, gain):
        ssq = jnp.dot((xh * xh).astype(BF16), seg, preferred_element_type=F32)
        xn = xh * lax.rsqrt(ssq * (1.0 / DA_HEAD_DIM) + NORM_EPS) * gain
        nxt = pltpu.roll(xn, HEAD_WIDTH - ROT_HALF, 1)
        prv = pltpu.roll(xn, ROT_HALF, 1)
        return xn * cos + nxt * sin_lo + prv * sin_hi

    scale = math.log2(math.e) / math.sqrt(DA_HEAD_DIM)
    for h in range(DA_HEADS):
        cols = slice(h * HEAD_WIDTH, (h + 1) * HEAD_WIDTH)
        q = norm_rope(qkv[:, cols], qg_ref[...])
        q_ref[:, cols] = (q * scale).astype(BF16)
        kcols = slice(D_MODEL + h * HEAD_WIDTH, D_MODEL + (h + 1) * HEAD_WIDTH)
        k_ref[:, cols] = norm_rope(qkv[:, kcols], kg_ref[...]).astype(BF16)
        vcols = slice(2 * D_MODEL + h * HEAD_WIDTH, 2 * D_MODEL + (h + 1) * HEAD_WIDTH)
        vt_ref[0, h, 0, :HEAD_WIDTH, :] = qkv[:, vcols].T.astype(BF16)
        vt_ref[0, h, 0, HEAD_WIDTH:, :] = jnp.ones((VT_ROWS - HEAD_WIDTH, TM_QKV), BF16)


def _qkv(x, pos, gain, w_qkv, freq_lane, q_gain, k_gain, seg_ones, batch, seq):
    n = x.shape[0]
    tiles = seq // TM_QKV
    per_key_tile = TK // TM_QKV
    tok = lambda i: (i, 0)
    out = jax.ShapeDtypeStruct((n, D_MODEL), BF16)
    return pl.pallas_call(
        _qkv_kernel,
        grid=(n // TM_QKV,),
        in_specs=[
            pl.BlockSpec((TM_QKV, D_MODEL), tok),
            pl.BlockSpec((TM_QKV, 1), tok),
            _const_spec((1, D_MODEL)),
            _const_spec((D_MODEL, 3 * D_MODEL)),
            _const_spec((1, HEAD_WIDTH)),
            _const_spec((1, HEAD_WIDTH)),
            _const_spec((1, HEAD_WIDTH)),
            _const_spec((HEAD_WIDTH, HEAD_WIDTH)),
        ],
        out_specs=[
            pl.BlockSpec((TM_QKV, D_MODEL), tok),
            pl.BlockSpec((TM_QKV, D_MODEL), tok),
            pl.BlockSpec((1, DA_HEADS, 1, VT_ROWS, TM_QKV),
                         lambda i: (i // tiles, 0, (i % tiles) // per_key_tile, 0,
                                    i % per_key_tile)),
        ],
        out_shape=[out, out,
                   jax.ShapeDtypeStruct((batch, DA_HEADS, seq // TK, VT_ROWS, TK), BF16)],
        compiler_params=_params("parallel"),
        name="qkv_norm_rope",
    )(x, pos, gain, w_qkv, freq_lane, q_gain, k_gain, seg_ones)


def _attn_kernel(lam_ref, sub_ref, q_ref, k_ref, vt_ref, wg_ref, wu_ref, wd_ref,
                 o_ref, wg_bf_ref, wu_bf_ref, wd_bf_ref,
                 qqt_ref, m_ref, acc_ref, sc_ref, pc_ref, ac_ref):
    for src, dst in ((wg_ref, wg_bf_ref), (wu_ref, wu_bf_ref), (wd_ref, wd_bf_ref)):
        dst[...] = src[...].astype(BF16)
    qi = pl.program_id(2)
    qt = q_ref[0].astype(F32).T
    first = lax.broadcasted_iota(I32, (HEAD_WIDTH, TQ), 0) < DA_HEAD_DIM
    qqt_ref[:, :TQ] = jnp.where(first, qt, 0.0).astype(BF16)
    qqt_ref[:, TQ:] = jnp.where(first, 0.0, qt).astype(BF16)
    m_ref[...] = jnp.full(m_ref.shape, -jnp.inf, F32)
    acc_ref[...] = jnp.zeros(acc_ref.shape, F32)

    n_chunks = 2 * TQ // QC
    chunk_cols = [slice(c * QC, (c + 1) * QC) for c in range(n_chunks)]

    def scores(k, cols):
        return jnp.dot(k, qqt_ref[:, cols], preferred_element_type=F32)

    visible = (lax.broadcasted_iota(I32, (QC, QC), 0)
               <= lax.broadcasted_iota(I32, (QC, QC), 1))

    def softmax(s, cols, q_offset):
        if q_offset is not None:
            tail = jnp.where(visible, s[q_offset:], -jnp.inf)
            s = jnp.concatenate([s[:q_offset], tail], axis=0) if q_offset else tail
        m_old = m_ref[:, cols]
        m_new = jnp.maximum(m_old, jnp.max(s, axis=0, keepdims=True))
        alpha = jnp.exp2(m_old - m_new)
        p = jnp.exp2((s - m_new).astype(BF16))
        m_ref[:, cols] = m_new
        return p, alpha

    def accumulate(p, alpha, vt, cols):
        acc_ref[:, cols] = alpha * acc_ref[:, cols] + jnp.dot(
            vt, p, preferred_element_type=F32)

    def k_block(j):
        return k_ref[0, pl.ds(pl.multiple_of(j * TK, TK), TK), :]

    def full_block(j, carry):
        s_cur, p_prev, alpha_prev = sc_ref[...], pc_ref[...], ac_ref[...]
        k = k_block(j)
        vt = vt_ref[0, 0, j]
        for c in range(n_chunks):
            if c + 1 < n_chunks:
                s_next = scores(k, chunk_cols[c + 1])
            else:
                sc_ref[...] = scores(k_block(j + 1), chunk_cols[0])
            p, alpha = softmax(s_cur, chunk_cols[c], None)
            vt_prev = vt_ref[0, 0, jnp.maximum(j - 1, 0)] if c == 0 else vt
            accumulate(p_prev, alpha_prev, vt_prev, chunk_cols[c - 1])
            s_cur, p_prev, alpha_prev = s_next, p, alpha
        pc_ref[...] = p_prev
        ac_ref[...] = alpha_prev
        return carry

    sc_ref[...] = scores(k_block(0), chunk_cols[0])
    pc_ref[...] = jnp.zeros(pc_ref.shape, BF16)
    ac_ref[...] = jnp.ones(ac_ref.shape, F32)
    lax.fori_loop(0, qi, full_block, 0)

    k_diag = k_block(qi)
    vt_diag = vt_ref[0, 0, qi]
    q_offsets = [(c * QC) % TQ for c in range(n_chunks)]
    nkeys = [q0 + QC for q0 in q_offsets]
    s_cur, p_prev, alpha_prev = sc_ref[:nkeys[0], :], pc_ref[...], ac_ref[...]
    vt_prev = vt_ref[0, 0, jnp.maximum(qi - 1, 0)]
    for c in range(n_chunks):
        if c + 1 < n_chunks:
            s_next = scores(k_diag[:nkeys[c + 1]], chunk_cols[c + 1])
        p, alpha = softmax(s_cur, chunk_cols[c], q_offsets[c])
        accumulate(p_prev, alpha_prev, vt_prev, chunk_cols[c - 1])
        s_cur, p_prev, alpha_prev, vt_prev = s_next, p, alpha, vt_diag[:, :nkeys[c]]
    accumulate(p_prev, alpha_prev, vt_prev, chunk_cols[n_chunks - 1])

    lam_terms = lam_ref[...]
    lam = (jnp.exp(jnp.sum(lam_terms[0:1] * lam_terms[1:2]))
           - jnp.exp(jnp.sum(lam_terms[2:3] * lam_terms[3:4])) + LAMBDA_INIT)
    pv = acc_ref[:HEAD_WIDTH, :]
    l = acc_ref[HEAD_WIDTH:HEAD_WIDTH + 1, :]
    ot = pv[:, :TQ] / l[:, :TQ] - lam * (pv[:, TQ:] / l[:, TQ:])
    o_ref[0] = (_rms(ot.T, sub_ref[...]) * (1.0 - LAMBDA_INIT)).astype(o_ref.dtype)


def _attention(q, k, vt, lam_terms, subln, batch, seq, expert_weights):
    q = q.reshape(batch, seq, D_MODEL)
    k = k.reshape(batch, seq, D_MODEL)
    q_tiles = seq // TQ
    steps = batch * DA_HEADS * q_tiles
    q_spec = pl.BlockSpec((1, TQ, HEAD_WIDTH), lambda b, h, i: (b, i, h))
    step_rows = lambda b, h, i: ((b * DA_HEADS + h) * q_tiles + i, 0)
    flat = [w.reshape(-1, w.shape[-1]) for w in expert_weights]
    cast_specs = [pl.BlockSpec((w.shape[0] // steps, w.shape[1]), step_rows) for w in flat]
    assert all(w.shape[0] % (16 * steps) == 0 for w in flat)
    o, *cast = pl.pallas_call(
        _attn_kernel,
        grid=(batch, DA_HEADS, q_tiles),
        in_specs=[
            _const_spec((4, DA_HEAD_DIM)),
            _const_spec((1, HEAD_WIDTH)),
            q_spec,
            pl.BlockSpec((1, seq, HEAD_WIDTH), lambda b, h, i: (b, 0, h)),
            pl.BlockSpec((1, 1, seq // TK, VT_ROWS, TK), lambda b, h, i: (b, h, 0, 0, 0)),
        ] + cast_specs,
        out_specs=[q_spec] + cast_specs,
        out_shape=[jax.ShapeDtypeStruct((batch, seq, D_MODEL), BF16)]
        + [jax.ShapeDtypeStruct(w.shape, BF16) for w in flat],
        scratch_shapes=[
            pltpu.VMEM((HEAD_WIDTH, 2 * TQ), BF16),
            pltpu.VMEM((1, 2 * TQ), F32),
            pltpu.VMEM((VT_ROWS, 2 * TQ), F32),
            pltpu.VMEM((TK, QC), F32),
            pltpu.VMEM((TK, QC), BF16),
            pltpu.VMEM((1, QC), F32),
        ],
        compiler_params=_params("parallel", "parallel", "arbitrary"),
        name="diff_attention",
    )(lam_terms, subln, q, k, vt, *flat)
    cast = [c.reshape(w.shape) for c, w in zip(cast, expert_weights)]
    return o.reshape(batch * seq, D_MODEL), cast


def _proj_route_kernel(o_ref, h_ref, wo_ref, g_ref, wrh_ref, wrl_ref,
                       h2_ref, hn_ref, ri_ref, rw_ref, cnt_ref, run_ref):
    @pl.when(pl.program_id(0) == 0)
    def _():
        run_ref[...] = jnp.zeros(run_ref.shape, F32)

    h2 = h_ref[...] + jnp.dot(o_ref[...], wo_ref[...], preferred_element_type=F32)
    h2_ref[...] = h2
    hn = _rms(h2, g_ref[...])
    _store_slabs(hn_ref, hn)
    hn_hi = hn.astype(BF16)
    hn_lo = (hn - hn_hi.astype(F32)).astype(BF16)
    logits = (jnp.dot(hn_hi, wrh_ref[...], preferred_element_type=F32)
              + jnp.dot(hn_lo, wrh_ref[...], preferred_element_type=F32)
              + jnp.dot(hn_hi, wrl_ref[...], preferred_element_type=F32))
    lane = lax.broadcasted_iota(I32, logits.shape, 1)
    logits = jnp.where(lane < N_EXPERTS, logits, -jnp.inf)
    m1 = jnp.max(logits, axis=-1, keepdims=True)
    i1 = jnp.min(jnp.where(logits == m1, lane, LANES), axis=-1, keepdims=True)
    rest = jnp.where(lane == i1, -jnp.inf, logits)
    m2 = jnp.max(rest, axis=-1, keepdims=True)
    i2 = jnp.min(jnp.where(rest == m2, lane, LANES), axis=-1, keepdims=True)
    e2 = jnp.exp(m2 - m1)
    denom = 1.0 + e2
    rw_ref[...] = jnp.where(lane == 0, 1.0 / denom, jnp.where(lane == 1, e2 / denom, 0.0))

    sel1 = lane == i1
    sel2 = lane == i2
    sel = jnp.where(sel1 | sel2, 1.0, 0.0)
    t_row = lax.broadcasted_iota(I32, (TM_PROJ, TM_PROJ), 0)
    t_col = lax.broadcasted_iota(I32, (TM_PROJ, TM_PROJ), 1)
    earlier = jnp.where(t_col < t_row, 1.0, 0.0).astype(BF16)
    rank = run_ref[...] + jnp.dot(earlier, sel.astype(BF16), preferred_element_type=F32)
    rank1 = jnp.sum(jnp.where(sel1, rank, 0.0), axis=-1, keepdims=True).astype(I32)
    rank2 = jnp.sum(jnp.where(sel2, rank, 0.0), axis=-1, keepdims=True).astype(I32)
    ri = jnp.where(lane == 0, i1, jnp.where(lane == 1, i2,
                   jnp.where(lane == 2, rank1, jnp.where(lane == 3, rank2, 0))))
    ri_ref[...] = ri.T[:8, :]
    run = run_ref[...] + jnp.sum(sel, axis=0, keepdims=True)
    run_ref[...] = run
    cnt_ref[...] = jnp.broadcast_to(run, cnt_ref.shape)


def _proj_route(o, h, w_out, gain, w_router_hi, w_router_lo):
    n = h.shape[0]
    tok = lambda i: (i, 0)
    return pl.pallas_call(
        _proj_route_kernel,
        grid=(n // TM_PROJ,),
        in_specs=[
            pl.BlockSpec((TM_PROJ, D_MODEL), tok),
            pl.BlockSpec((TM_PROJ, D_MODEL), tok),
            _const_spec((D_MODEL, D_MODEL)),
            _const_spec((1, D_MODEL)),
            _const_spec((D_MODEL, LANES)),
            _const_spec((D_MODEL, LANES)),
        ],
        out_specs=[
            pl.BlockSpec((TM_PROJ, D_MODEL), tok),
            pl.BlockSpec((TM_PROJ * SLAB, LANES), tok),
            pl.BlockSpec((8, TM_PROJ), lambda i: (0, i)),
            pl.BlockSpec((TM_PROJ, LANES), tok),
            pl.BlockSpec((8, LANES), lambda i: (0, 0)),
        ],
        out_shape=[
            jax.ShapeDtypeStruct((n, D_MODEL), F32),
            jax.ShapeDtypeStruct((n * SLAB, LANES), F32),
            jax.ShapeDtypeStruct((8, n), I32),
            jax.ShapeDtypeStruct((n, LANES), F32),
            jax.ShapeDtypeStruct((8, LANES), F32),
        ],
        scratch_shapes=[pltpu.VMEM((1, LANES), F32)],
        compiler_params=_params("arbitrary"),
        name="proj_route",
    )(o, h, w_out, gain, w_router_hi, w_router_lo)


def _row_copy(src_ref, src_row, dst_ref, dst_row, sem):
    return pltpu.make_async_copy(src_ref.at[_slab_rows(src_row)], dst_ref.at[_slab_rows(dst_row)], sem)


def _dispatch_kernel(d0_ref, d1_ref, zrow_ref, hn_ref, xs_ref, zero_ref, zsem, sem):
    i = pl.program_id(0)

    def zero_fill(z):
        rows = pl.ds(pl.multiple_of(zrow_ref[z] * SLAB, SLAB), TM_GROUP * SLAB)
        return pltpu.make_async_copy(zero_ref, xs_ref.at[rows], zsem)

    @pl.when(i == 0)
    def _():
        zero_ref[...] = jnp.zeros(zero_ref.shape, F32)
        for z in range(2 * N_EXPERTS):
            @pl.when(zrow_ref[z] >= 0)
            def _():
                zero_fill(z).start()
        for z in range(2 * N_EXPERTS):
            @pl.when(zrow_ref[z] >= 0)
            def _():
                zero_fill(z).wait()

    base = i * TM_MOVE

    def issue(t, carry):
        _row_copy(hn_ref, t, xs_ref, d0_ref[base + t], sem).start(priority=0)
        _row_copy(hn_ref, t, xs_ref, d1_ref[base + t], sem).start(priority=1)
        return carry

    lax.fori_loop(0, TM_MOVE, issue, 0, unroll=8)

    def drain(t, carry):
        _row_copy(hn_ref, t, xs_ref, d0_ref[base + t], sem).wait()
        _row_copy(hn_ref, t, xs_ref, d1_ref[base + t], sem).wait()
        return carry

    lax.fori_loop(0, TM_MOVE, drain, 0, unroll=8)


def _dispatch(d0, d1, zero_rows, hn, rows):
    n = hn.shape[0] // SLAB
    return pl.pallas_call(
        _dispatch_kernel,
        grid_spec=pltpu.PrefetchScalarGridSpec(
            num_scalar_prefetch=3,
            grid=(n // TM_MOVE,),
            in_specs=[pl.BlockSpec((TM_MOVE * SLAB, LANES), lambda i, d0, d1, z: (i, 0))],
            out_specs=pl.BlockSpec(memory_space=pl.ANY),
            scratch_shapes=[
                pltpu.VMEM((TM_GROUP * SLAB, LANES), F32),
                pltpu.SemaphoreType.DMA,
                pltpu.SemaphoreType.DMA,
            ],
        ),
        out_shape=jax.ShapeDtypeStruct((rows * SLAB, LANES), F32),
        compiler_params=_params("arbitrary"),
        name="moe_dispatch",
    )(d0, d1, zero_rows, hn)


def _group_ffn_kernel(te_ref, tv_ref, x_ref, wg_ref, wu_ref, wd_ref, o_ref, xb_ref, acc_ref):
    g = pl.program_id(0)
    j = pl.program_id(1)

    @pl.when((tv_ref[g] == 0) & (j == 0))
    def _():
        o_ref[...] = jnp.zeros(o_ref.shape, F32)

    @pl.when(tv_ref[g] > 0)
    def _():
        @pl.when(j == 0)
        def _():
            for s in range(SLAB):
                xb_ref[:, s * LANES:(s + 1) * LANES] = _load_slab_cols(
                    x_ref, s, TM_GROUP).astype(BF16)
            acc_ref[...] = jnp.zeros(acc_ref.shape, F32)

        x = xb_ref[...]
        gate = jnp.dot(x, wg_ref[0], preferred_element_type=F32)
        up = jnp.dot(x, wu_ref[0], preferred_element_type=F32)
        act = (jax.nn.silu(gate) * up).astype(BF16)
        acc_ref[...] += jnp.dot(act, wd_ref[0], preferred_element_type=F32)

        @pl.when(j == pl.num_programs(1) - 1)
        def _():
            _store_slabs(o_ref, acc_ref[...])


def _group_ffn(tile_expert, tile_valid, xs, w_gate, w_up, w_down):
    rows = xs.shape[0] // SLAB
    n_chunks = FFN_EXPERT // FFN_EXPERT_CHUNK
    last = n_chunks - 1
    chunk = lambda g, j, tv: jnp.where(g % 2 == 0, j, last - j)
    row_map = lambda g, j, te, tv: (g, 0)
    return pl.pallas_call(
        _group_ffn_kernel,
        grid_spec=pltpu.PrefetchScalarGridSpec(
            num_scalar_prefetch=2,
            grid=(rows // TM_GROUP, n_chunks),
            in_specs=[
                pl.BlockSpec((TM_GROUP * SLAB, LANES), row_map),
                pl.BlockSpec((1, D_MODEL, FFN_EXPERT_CHUNK),
                             lambda g, j, te, tv: (te[g], 0, chunk(g, j, tv))),
                pl.BlockSpec((1, D_MODEL, FFN_EXPERT_CHUNK),
                             lambda g, j, te, tv: (te[g], 0, chunk(g, j, tv))),
                pl.BlockSpec((1, FFN_EXPERT_CHUNK, D_MODEL),
                             lambda g, j, te, tv: (te[g], chunk(g, j, tv), 0)),
            ],
            out_specs=pl.BlockSpec((TM_GROUP * SLAB, LANES), row_map),
            scratch_shapes=[
                pltpu.VMEM((TM_GROUP, D_MODEL), BF16),
                pltpu.VMEM((TM_GROUP, D_MODEL), F32),
            ],
        ),
        out_shape=jax.ShapeDtypeStruct((rows * SLAB, LANES), F32),
        compiler_params=_params("arbitrary", "arbitrary"),
        name="moe_group_ffn",
    )(tile_expert, tile_valid, xs, w_gate, w_up, w_down)


def _combine_kernel(d0_ref, d1_ref, h_ref, rw_ref, ys_ref, o_ref, r0_ref, r1_ref, sems):
    i = pl.program_id(0)
    slot = i % 2

    def gather(tile, buf, action):
        base = tile * TM_MOVE
        r0, r1, sem = r0_ref.at[buf], r1_ref.at[buf], sems.at[buf]

        def body(t, carry):
            c0 = _row_copy(ys_ref, d0_ref[base + t], r0, t, sem)
            c1 = _row_copy(ys_ref, d1_ref[base + t], r1, t, sem)
            if action == "start":
                c0.start(priority=0)
                c1.start(priority=1)
            else:
                c0.wait()
                c1.wait()
            return carry

        lax.fori_loop(0, TM_MOVE, body, 0, unroll=8)

    @pl.when(i == 0)
    def _():
        gather(0, 0, "start")

    @pl.when(i + 1 < pl.num_programs(0))
    def _():
        gather(i + 1, 1 - slot, "start")

    gather(i, slot, "wait")
    rw = rw_ref[...]
    w1 = rw[:, 0:1]
    w2 = rw[:, 1:2]
    r0, r1 = r0_ref.at[slot], r1_ref.at[slot]
    for s in range(SLAB):
        cols = slice(s * LANES, (s + 1) * LANES)
        o_ref[:, cols] = (h_ref[:, cols] + w1 * _load_slab_cols(r0, s, TM_MOVE)
                          + w2 * _load_slab_cols(r1, s, TM_MOVE))


def _combine(d0, d1, h, route_w, ys):
    n = h.shape[0]
    tok = lambda i, d0, d1: (i, 0)
    return pl.pallas_call(
        _combine_kernel,
        grid_spec=pltpu.PrefetchScalarGridSpec(
            num_scalar_prefetch=2,
            grid=(n // TM_MOVE,),
            in_specs=[
                pl.BlockSpec((TM_MOVE, D_MODEL), tok),
                pl.BlockSpec((TM_MOVE, LANES), tok),
                pl.BlockSpec(memory_space=pl.ANY),
            ],
            out_specs=pl.BlockSpec((TM_MOVE, D_MODEL), tok),
            scratch_shapes=[
                pltpu.VMEM((2, TM_MOVE * SLAB, LANES), F32),
                pltpu.VMEM((2, TM_MOVE * SLAB, LANES), F32),
                pltpu.SemaphoreType.DMA((2,)),
            ],
        ),
        out_shape=jax.ShapeDtypeStruct((n, D_MODEL), F32),
        compiler_params=_params("arbitrary"),
        name="moe_combine",
    )(d0, d1, h, route_w, ys)


def _routing_tables(route_i, counts, n):
    counts = counts.astype(I32)
    padded = (counts + TM_GROUP - 1) // TM_GROUP * TM_GROUP
    ends = jnp.cumsum(padded)
    starts = ends - padded
    d0 = starts[route_i[0]] + route_i[2]
    d1 = starts[route_i[1]] + route_i[3]
    n_tiles = TOP_K * n // TM_GROUP + N_EXPERTS
    used = ends[-1] // TM_GROUP
    g = jnp.arange(n_tiles, dtype=I32)
    first_row = jnp.minimum(g, used - 1) * TM_GROUP
    tile_expert = jnp.sum((ends[None, :] <= first_row[:, None]).astype(I32), axis=1)
    tile_valid = (g < used).astype(I32)
    tail = g[-N_EXPERTS:]
    zero_rows = jnp.concatenate([
        jnp.where(padded > 0, ends - TM_GROUP, -1),
        jnp.where(tail >= used, tail * TM_GROUP, -1)]).astype(I32)
    return d0, d1, zero_rows, tile_expert, tile_valid, n_tiles * TM_GROUP


def kernel(x, positions, l0_mix_norm, l0_sg_w_in, l0_sg_v_norm, l0_sg_w_spatial, l0_sg_b_spatial, l0_sg_w_out, l0_ffn_norm, l0_ffn_w_gate, l0_ffn_w_up, l0_ffn_w_down, l1_mix_norm, l1_da_w_qkv, l1_da_q_norm, l1_da_k_norm, l1_da_lambda_q1, l1_da_lambda_k1, l1_da_lambda_q2, l1_da_lambda_k2, l1_da_subln, l1_da_w_out, l1_moe_norm, l1_moe_w_router, l1_moe_w_gate, l1_moe_w_up, l1_moe_w_down):
    batch, seq, _ = x.shape
    n = batch * seq
    row = lambda a: a.reshape(1, -1)
    h = x.reshape(n, D_MODEL)

    h = _sg_mixer(h, row(l0_mix_norm), l0_sg_w_in.astype(BF16), row(l0_sg_v_norm),
                  l0_sg_w_spatial, l0_sg_b_spatial.T, l0_sg_w_out.astype(BF16))
    h = _dense_ffn(h, row(l0_ffn_norm), l0_ffn_w_gate.astype(BF16),
                   l0_ffn_w_up.astype(BF16), l0_ffn_w_down.astype(BF16))

    inv_freq = 1.0 / (ROPE_THETA ** (jnp.arange(0, ROT_DIM, 2, dtype=F32) / ROT_DIM))
    d = jnp.arange(HEAD_WIDTH) % DA_HEAD_DIM
    freq_lane = jnp.where(d < ROT_DIM, inv_freq[d % ROT_HALF], 0.0).reshape(1, HEAD_WIDTH)
    seg = jnp.arange(HEAD_WIDTH) // DA_HEAD_DIM
    seg_ones = (seg[:, None] == seg[None, :]).astype(BF16)
    q, k, vt = _qkv(h, positions.reshape(n, 1), row(l1_mix_norm), l1_da_w_qkv.astype(BF16),
                    freq_lane, row(jnp.tile(l1_da_q_norm, 2)), row(jnp.tile(l1_da_k_norm, 2)),
                    seg_ones, batch, seq)
    lam_terms = jnp.stack([l1_da_lambda_q1, l1_da_lambda_k1, l1_da_lambda_q2, l1_da_lambda_k2])
    o, (w_gate, w_up, w_down) = _attention(
        q, k, vt, lam_terms, row(l1_da_subln), batch, seq,
        (l1_moe_w_gate, l1_moe_w_up, l1_moe_w_down))

    w_router = jnp.pad(l1_moe_w_router, ((0, 0), (0, LANES - N_EXPERTS)))
    w_router_hi = w_router.astype(BF16)
    w_router_lo = (w_router - w_router_hi.astype(F32)).astype(BF16)
    h, hn, route_i, route_w, counts = _proj_route(
        o, h, l1_da_w_out.astype(BF16), row(l1_moe_norm), w_router_hi, w_router_lo)
    d0, d1, zero_rows, tile_expert, tile_valid, rows = _routing_tables(
        route_i, counts[0, :N_EXPERTS], n)
    xs = _dispatch(d0, d1, zero_rows, hn, rows)
    ys = _group_ffn(tile_expert, tile_valid, xs, w_gate, w_up, w_down)
    h = _combine(d0, d1, h, route_w, ys)
    return h.reshape(batch, seq, D_MODEL)
```

```python
import math

import jax
import jax.numpy as jnp
from jax import lax
from jax.experimental import pallas as pl
from jax.experimental.pallas import tpu as pltpu

F32 = jnp.float32
BF16 = jnp.bfloat16
I32 = jnp.int32

D_MODEL = 1024
SG_WIDTH = 2 * D_MODEL
SG_GROUPS = 8
SG_CHUNK = 128
SG_GROUP_WIDTH = SG_WIDTH // SG_GROUPS
DA_HEADS = 8
DA_HEAD_DIM = 64
HEAD_WIDTH = 2 * DA_HEAD_DIM
ROT_DIM = DA_HEAD_DIM // 4
ROT_HALF = ROT_DIM // 2
ROPE_THETA = 500000.0
FFN_DENSE = 2816
N_EXPERTS = 8
TOP_K = 2
FFN_EXPERT = 3584
NORM_EPS = 1e-6
LAMBDA_INIT = 0.8 - 0.6 * math.exp(-0.3 * 1)

LANES = 128
VMEM_LIMIT = 56 * 1024 * 1024

TM_SG = 512
TM_FFN = 512
TM_QKV = 512
TK = 1024
TQ = 1024
QC = 256
VT_ROWS = HEAD_WIDTH + 16
TM_PROJ = 512
TM_GROUP = 512
FFN_EXPERT_CHUNK = 1792
TM_MOVE = 1024
SLAB = D_MODEL // LANES
assert SLAB == 8


def _slab_rows(t):
    return pl.ds(pl.multiple_of(t * SLAB, SLAB), SLAB)


def _load_slab_cols(ref, s, tokens):
    return ref[pl.ds(s, tokens, stride=SLAB), :]


def _store_slabs(ref, x):
    for s in range(SLAB):
        ref[pl.ds(s, x.shape[0], stride=SLAB), :] = x[:, s * LANES:(s + 1) * LANES]


def _rms(x, gain):
    return x * lax.rsqrt(jnp.mean(x * x, axis=-1, keepdims=True) + NORM_EPS) * gain


def _const_spec(shape):
    zeros = (0,) * len(shape)
    return pl.BlockSpec(shape, lambda *_: zeros, pipeline_mode=pl.Buffered(1))


def _params(*semantics):
    return pltpu.CompilerParams(dimension_semantics=semantics, vmem_limit_bytes=VMEM_LIMIT)


def _sg_mixer_kernel(x_ref, g_ref, win_ref, vg_ref, wsp_ref, bt_ref, wout_ref, o_ref, y_ref):
    x = x_ref[...]
    hn = _rms(x, g_ref[...]).astype(BF16)
    z = jax.nn.gelu(jnp.dot(hn, win_ref[...], preferred_element_type=F32))
    u = z[:, :SG_WIDTH]
    v = _rms(z[:, SG_WIDTH:], vg_ref[...]).astype(BF16)
    row = lax.broadcasted_iota(I32, (SG_CHUNK, SG_CHUNK), 0)
    col = lax.broadcasted_iota(I32, (SG_CHUNK, SG_CHUNK), 1)
    causal = col <= row
    for g in range(SG_GROUPS):
        w_masked = jnp.where(causal, wsp_ref[g], 0.0).astype(BF16)
        bias = bt_ref[:, g:g + 1]
        cols = slice(g * SG_GROUP_WIDTH, (g + 1) * SG_GROUP_WIDTH)
        for c in range(TM_SG // SG_CHUNK):
            rows = slice(c * SG_CHUNK, (c + 1) * SG_CHUNK)
            mixed = jnp.dot(w_masked, v[rows, cols], preferred_element_type=F32) + bias
            y_ref[rows, cols] = (u[rows, cols] * mixed).astype(BF16)
    o_ref[...] = x + jnp.dot(y_ref[...], wout_ref[...], preferred_element_type=F32)


def _sg_mixer(x, gain, w_in, v_gain, w_spatial, b_spatial_t, w_out):
    n = x.shape[0]
    tok = lambda i: (i, 0)
    return pl.pallas_call(
        _sg_mixer_kernel,
        grid=(n // TM_SG,),
        in_specs=[
            pl.BlockSpec((TM_SG, D_MODEL), tok),
            _const_spec((1, D_MODEL)),
            _const_spec((D_MODEL, 2 * SG_WIDTH)),
            _const_spec((1, SG_WIDTH)),
            _const_spec((SG_GROUPS, SG_CHUNK, SG_CHUNK)),
            _const_spec((SG_CHUNK, SG_GROUPS)),
            _const_spec((SG_WIDTH, D_MODEL)),
        ],
        out_specs=pl.BlockSpec((TM_SG, D_MODEL), tok),
        out_shape=jax.ShapeDtypeStruct((n, D_MODEL), F32),
        scratch_shapes=[pltpu.VMEM((TM_SG, SG_WIDTH), BF16)],
        compiler_params=_params("parallel"),
        name="sg_mixer",
    )(x, gain, w_in, v_gain, w_spatial, b_spatial_t, w_out)


def _dense_ffn_kernel(x_ref, g_ref, wg_ref, wu_ref, wd_ref, o_ref):
    x = x_ref[...]
    hn = _rms(x, g_ref[...]).astype(BF16)
    gate = jnp.dot(hn, wg_ref[...], preferred_element_type=F32)
    up = jnp.dot(hn, wu_ref[...], preferred_element_type=F32)
    act = (jax.nn.silu(gate) * up).astype(BF16)
    o_ref[...] = x + jnp.dot(act, wd_ref[...], preferred_element_type=F32)


def _dense_ffn(x, gain, w_gate, w_up, w_down):
    n = x.shape[0]
    tok = lambda i: (i, 0)
    return pl.pallas_call(
        _dense_ffn_kernel,
        grid=(n // TM_FFN,),
        in_specs=[
            pl.BlockSpec((TM_FFN, D_MODEL), tok),
            _const_spec((1, D_MODEL)),
            _const_spec((D_MODEL, FFN_DENSE)),
            _const_spec((D_MODEL, FFN_DENSE)),
            _const_spec((FFN_DENSE, D_MODEL)),
        ],
        out_specs=pl.BlockSpec((TM_FFN, D_MODEL), tok),
        out_shape=jax.ShapeDtypeStruct((n, D_MODEL), F32),
        compiler_params=_params("parallel"),
        name="dense_ffn",
    )(x, gain, w_gate, w_up, w_down)


def _qkv_kernel(x_ref, pos_ref, g_ref, w_ref, freq_ref, qg_ref, kg_ref, seg_ref,
                q_ref, k_ref, vt_ref):
    hn = _rms(x_ref[...], g_ref[...]).astype(BF16)
    qkv = jnp.dot(hn, w_ref[...], preferred_element_type=F32)

    ang = pos_ref[...].astype(F32) * freq_ref[...]
    cos = jnp.cos(ang)
    sin = jnp.sin(ang)
    d = lax.broadcasted_iota(I32, (1, HEAD_WIDTH), 1) % DA_HEAD_DIM
    sin_lo = jnp.where(d < ROT_HALF, -sin, 0.0)
    sin_hi = jnp.where((d >= ROT_HALF) & (d < ROT_DIM), sin, 0.0)
    seg = seg_ref[...]

    def norm_rope(xh, gain):
        ssq = jnp.dot((xh * xh).astype(BF16), seg, preferred_element_type=F32)
        xn = xh * lax.rsqrt(ssq * (1.0 / DA_HEAD_DIM) + NORM_EPS) * gain
        nxt = pltpu.roll(xn, HEAD_WIDTH - ROT_HALF, 1)
        prv = pltpu.roll(xn, ROT_HALF, 1)
        return xn * cos + nxt * sin_lo + prv * sin_hi

    scale = math.log2(math.e) / math.sqrt(DA_HEAD_DIM)
    for h in range(DA_HEADS):
        cols = slice(h * HEAD_WIDTH, (h + 1) * HEAD_WIDTH)
        q = norm_rope(qkv[:, cols], qg_ref[...])
        q_ref[:, cols] = (q * scale).astype(BF16)
        kcols = slice(D_MODEL + h * HEAD_WIDTH, D_MODEL + (h + 1) * HEAD_WIDTH)
        k_ref[:, cols] = norm_rope(qkv[:, kcols], kg_ref[...]).astype(BF16)
        vcols = slice(2 * D_MODEL + h * HEAD_WIDTH, 2 * D_MODEL + (h + 1) * HEAD_WIDTH)
        vt_ref[0, h, 0, :HEAD_WIDTH, :] = qkv[:, vcols].T.astype(BF16)
        vt_ref[0, h, 0, HEAD_WIDTH:, :] = jnp.ones((VT_ROWS - HEAD_WIDTH, TM_QKV), BF16)


def _qkv(x, pos, gain, w_qkv, freq_lane, q_gain, k_gain, seg_ones, batch, seq):
    n = x.shape[0]
    tiles = seq // TM_QKV
    per_key_tile = TK // TM_QKV
    tok = lambda i: (i, 0)
    out = jax.ShapeDtypeStruct((n, D_MODEL), BF16)
    return pl.pallas_call(
        _qkv_kernel,
        grid=(n // TM_QKV,),
        in_specs=[
            pl.BlockSpec((TM_QKV, D_MODEL), tok),
            pl.BlockSpec((TM_QKV, 1), tok),
            _const_spec((1, D_MODEL)),
            _const_spec((D_MODEL, 3 * D_MODEL)),
            _const_spec((1, HEAD_WIDTH)),
            _const_spec((1, HEAD_WIDTH)),
            _const_spec((1, HEAD_WIDTH)),
            _const_spec((HEAD_WIDTH, HEAD_WIDTH)),
        ],
        out_specs=[
            pl.BlockSpec((TM_QKV, D_MODEL), tok),
            pl.BlockSpec((TM_QKV, D_MODEL), tok),
            pl.BlockSpec((1, DA_HEADS, 1, VT_ROWS, TM_QKV),
                         lambda i: (i // tiles, 0, (i % tiles) // per_key_tile, 0,
                                    i % per_key_tile)),
        ],
        out_shape=[out, out,
                   jax.ShapeDtypeStruct((batch, DA_HEADS, seq // TK, VT_ROWS, TK), BF16)],
        compiler_params=_params("parallel"),
        name="qkv_norm_rope",
    )(x, pos, gain, w_qkv, freq_lane, q_gain, k_gain, seg_ones)


def _attn_kernel(lam_ref, sub_ref, q_ref, k_ref, vt_ref, wg_ref, wu_ref, wd_ref,
                 o_ref, wg_bf_ref, wu_bf_ref, wd_bf_ref,
                 qqt_ref, m_ref, acc_ref, sc_ref, pc_ref, ac_ref):
    for src, dst in ((wg_ref, wg_bf_ref), (wu_ref, wu_bf_ref), (wd_ref, wd_bf_ref)):
        dst[...] = src[...].astype(BF16)
    qi = pl.program_id(2)
    qt = q_ref[0].astype(F32).T
    first = lax.broadcasted_iota(I32, (HEAD_WIDTH, TQ), 0) < DA_HEAD_DIM
    qqt_ref[:, :TQ] = jnp.where(first, qt, 0.0).astype(BF16)
    qqt_ref[:, TQ:] = jnp.where(first, 0.0, qt).astype(BF16)
    m_ref[...] = jnp.full(m_ref.shape, -jnp.inf, F32)
    acc_ref[...] = jnp.zeros(acc_ref.shape, F32)

    n_chunks = 2 * TQ // QC
    chunk_cols = [slice(c * QC, (c + 1) * QC) for c in range(n_chunks)]

    def scores(k, cols):
        return jnp.dot(k, qqt_ref[:, cols], preferred_element_type=F32)

    visible = (lax.broadcasted_iota(I32, (QC, QC), 0)
               <= lax.broadcasted_iota(I32, (QC, QC), 1))

    def softmax(s, cols, q_offset):
        if q_offset is not None:
            tail = jnp.where(visible, s[q_offset:], -jnp.inf)
            s = jnp.concatenate([s[:q_offset], tail], axis=0) if q_offset else tail
        m_old = m_ref[:, cols]
        m_new = jnp.maximum(m_old, jnp.max(s, axis=0, keepdims=True))
        alpha = jnp.exp2(m_old - m_new)
        p = jnp.exp2((s - m_new).astype(BF16))
        m_ref[:, cols] = m_new
        return p, alpha

    def accumulate(p, alpha, vt, cols):
        acc_ref[:, cols] = alpha * acc_ref[:, cols] + jnp.dot(
            vt, p, preferred_element_type=F32)

    def k_block(j):
        return k_ref[0, pl.ds(pl.multiple_of(j * TK, TK), TK), :]

    def full_block(j, carry):
        s_cur, p_prev, alpha_prev = sc_ref[...], pc_ref[...], ac_ref[...]
        k = k_block(j)
        vt = vt_ref[0, 0, j]
        for c in range(n_chunks):
            if c + 1 < n_chunks:
                s_next = scores(k, chunk_cols[c + 1])
            else:
                sc_ref[...] = scores(k_block(j + 1), chunk_cols[0])
            p, alpha = softmax(s_cur, chunk_cols[c], None)
            vt_prev = vt_ref[0, 0, jnp.maximum(j - 1, 0)] if c == 0 else vt
            accumulate(p_prev, alpha_prev, vt_prev, chunk_cols[c - 1])
            s_cur, p_prev, alpha_prev = s_next, p, alpha
        pc_ref[...] = p_prev
        ac_ref[...] = alpha_prev
        return carry

    sc_ref[...] = scores(k_block(0), chunk_cols[0])
    pc_ref[...] = jnp.zeros(pc_ref.shape, BF16)
    ac_ref[...] = jnp.ones(ac_ref.shape, F32)
    lax.fori_loop(0, qi, full_block, 0)

    k_diag = k_block(qi)
    vt_diag = vt_ref[0, 0, qi]
    q_offsets = [(c * QC) % TQ for c in range(n_chunks)]
    nkeys = [q0 + QC for q0 in q_offsets]
    s_cur, p_prev, alpha_prev = sc_ref[:nkeys[0], :], pc_ref[...], ac_ref[...]
    vt_prev = vt_ref[0, 0, jnp.maximum(qi - 1, 0)]
    for c in range(n_chunks):
        if c + 1 < n_chunks:
            s_next = scores(k_diag[:nkeys[c + 1]], chunk_cols[c + 1])
        p, alpha = softmax(s_cur, chunk_cols[c], q_offsets[c])
        accumulate(p_prev, alpha_prev, vt_prev, chunk_cols[c - 1])
        s_cur, p_prev, alpha_prev, vt_prev = s_next, p, alpha, vt_diag[:, :nkeys[c]]
    accumulate(p_prev, alpha_prev, vt_prev, chunk_cols[n_chunks - 1])

    lam_terms = lam_ref[...]
    lam = (jnp.exp(jnp.sum(lam_terms[0:1] * lam_terms[1:2]))
           - jnp.exp(jnp.sum(lam_terms[2:3] * lam_terms[3:4])) + LAMBDA_INIT)
    pv = acc_ref[:HEAD_WIDTH, :]
    l = acc_ref[HEAD_WIDTH:HEAD_WIDTH + 1, :]
    ot = pv[:, :TQ] / l[:, :TQ] - lam * (pv[:, TQ:] / l[:, TQ:])
    o_ref[0] = (_rms(ot.T, sub_ref[...]) * (1.0 - LAMBDA_INIT)).astype(o_ref.dtype)


def _attention(q, k, vt, lam_terms, subln, batch, seq, expert_weights):
    q = q.reshape(batch, seq, D_MODEL)
    k = k.reshape(batch, seq, D_MODEL)
    q_tiles = seq // TQ
    steps = batch * DA_HEADS * q_tiles
    q_spec = pl.BlockSpec((1, TQ, HEAD_WIDTH), lambda b, h, i: (b, i, h))
    step_rows = lambda b, h, i: ((b * DA_HEADS + h) * q_tiles + i, 0)
    flat = [w.reshape(-1, w.shape[-1]) for w in expert_weights]
    cast_specs = [pl.BlockSpec((w.shape[0] // steps, w.shape[1]), step_rows) for w in flat]
    assert all(w.shape[0] % (16 * steps) == 0 for w in flat)
    o, *cast = pl.pallas_call(
        _attn_kernel,
        grid=(batch, DA_HEADS, q_tiles),
        in_specs=[
            _const_spec((4, DA_HEAD_DIM)),
            _const_spec((1, HEAD_WIDTH)),
            q_spec,
            pl.BlockSpec((1, seq, HEAD_WIDTH), lambda b, h, i: (b, 0, h)),
            pl.BlockSpec((1, 1, seq // TK, VT_ROWS, TK), lambda b, h, i: (b, h, 0, 0, 0)),
        ] + cast_specs,
        out_specs=[q_spec] + cast_specs,
        out_shape=[jax.ShapeDtypeStruct((batch, seq, D_MODEL), BF16)]
        + [jax.ShapeDtypeStruct(w.shape, BF16) for w in flat],
        scratch_shapes=[
            pltpu.VMEM((HEAD_WIDTH, 2 * TQ), BF16),
            pltpu.VMEM((1, 2 * TQ), F32),
            pltpu.VMEM((VT_ROWS, 2 * TQ), F32),
            pltpu.VMEM((TK, QC), F32),
            pltpu.VMEM((TK, QC), BF16),
            pltpu.VMEM((1, QC), F32),
        ],
        compiler_params=_params("parallel", "parallel", "arbitrary"),
        name="diff_attention",
    )(lam_terms, subln, q, k, vt, *flat)
    cast = [c.reshape(w.shape) for c, w in zip(cast, expert_weights)]
    return o.reshape(batch * seq, D_MODEL), cast


def _proj_route_kernel(o_ref, h_ref, wo_ref, g_ref, wr_ref,
                       h2_ref, hn_ref, ri_ref, rw_ref, cnt_ref, run_ref):
    @pl.when(pl.program_id(0) == 0)
    def _():
        run_ref[...] = jnp.zeros(run_ref.shape, F32)

    h2 = h_ref[...] + jnp.dot(o_ref[...], wo_ref[...], preferred_element_type=F32)
    h2_ref[...] = h2
    hn = _rms(h2, g_ref[...])
    _store_slabs(hn_ref, hn)
    hn_hi = hn.astype(BF16)
    hn_lo = (hn - hn_hi.astype(F32)).astype(BF16)
    both = jnp.dot(hn_hi, wr_ref[...], preferred_element_type=F32)
    logits = (both[:, :LANES] + both[:, LANES:]
              + jnp.dot(hn_lo, wr_ref[:, :LANES], preferred_element_type=F32))
    lane = lax.broadcasted_iota(I32, logits.shape, 1)
    logits = jnp.where(lane < N_EXPERTS, logits, -jnp.inf)
    m1 = jnp.max(logits, axis=-1, keepdims=True)
    i1 = jnp.min(jnp.where(logits == m1, lane, LANES), axis=-1, keepdims=True)
    rest = jnp.where(lane == i1, -jnp.inf, logits)
    m2 = jnp.max(rest, axis=-1, keepdims=True)
    i2 = jnp.min(jnp.where(rest == m2, lane, LANES), axis=-1, keepdims=True)
    e2 = jnp.exp(m2 - m1)
    denom = 1.0 + e2
    rw_ref[...] = jnp.where(lane == 0, 1.0 / denom, jnp.where(lane == 1, e2 / denom, 0.0))

    sel1 = lane == i1
    sel2 = lane == i2
    sel = jnp.where(sel1 | sel2, 1.0, 0.0)
    t_row = lax.broadcasted_iota(I32, (TM_PROJ, TM_PROJ), 0)
    t_col = lax.broadcasted_iota(I32, (TM_PROJ, TM_PROJ), 1)
    earlier = jnp.where(t_col < t_row, 1.0, 0.0).astype(BF16)
    rank = run_ref[...] + jnp.dot(earlier, sel.astype(BF16), preferred_element_type=F32)
    rank1 = jnp.sum(jnp.where(sel1, rank, 0.0), axis=-1, keepdims=True).astype(I32)
    rank2 = jnp.sum(jnp.where(sel2, rank, 0.0), axis=-1, keepdims=True).astype(I32)
    ri = jnp.where(lane == 0, i1, jnp.where(lane == 1, i2,
                   jnp.where(lane == 2, rank1, jnp.where(lane == 3, rank2, 0))))
    ri_ref[...] = ri.T[:8, :]
    run = run_ref[...] + jnp.sum(sel, axis=0, keepdims=True)
    run_ref[...] = run
    cnt_ref[...] = jnp.broadcast_to(run, cnt_ref.shape)


def _proj_route(o, h, w_out, gain, w_router_hi_lo):
    n = h.shape[0]
    tok = lambda i: (i, 0)
    return pl.pallas_call(
        _proj_route_kernel,
        grid=(n // TM_PROJ,),
        in_specs=[
            pl.BlockSpec((TM_PROJ, D_MODEL), tok),
            pl.BlockSpec((TM_PROJ, D_MODEL), tok),
            _const_spec((D_MODEL, D_MODEL)),
            _const_spec((1, D_MODEL)),
            _const_spec((D_MODEL, 2 * LANES)),
        ],
        out_specs=[
            pl.BlockSpec((TM_PROJ, D_MODEL), tok),
            pl.BlockSpec((TM_PROJ * SLAB, LANES), tok),
            pl.BlockSpec((8, TM_PROJ), lambda i: (0, i)),
            pl.BlockSpec((TM_PROJ, LANES), tok),
            pl.BlockSpec((8, LANES), lambda i: (0, 0)),
        ],
        out_shape=[
            jax.ShapeDtypeStruct((n, D_MODEL), F32),
            jax.ShapeDtypeStruct((n * SLAB, LANES), F32),
            jax.ShapeDtypeStruct((8, n), I32),
            jax.ShapeDtypeStruct((n, LANES), F32),
            jax.ShapeDtypeStruct((8, LANES), F32),
        ],
        scratch_shapes=[pltpu.VMEM((1, LANES), F32)],
        compiler_params=_params("arbitrary"),
        name="proj_route",
    )(o, h, w_out, gain, w_router_hi_lo)


def _row_copy(src_ref, src_row, dst_ref, dst_row, sem):
    return pltpu.make_async_copy(src_ref.at[_slab_rows(src_row)], dst_ref.at[_slab_rows(dst_row)], sem)


def _dispatch_kernel(d0_ref, d1_ref, zrow_ref, hn_ref, xs_ref, zero_ref, zsem, sem):
    i = pl.program_id(0)

    def zero_fill(z):
        rows = pl.ds(pl.multiple_of(zrow_ref[z] * SLAB, SLAB), TM_GROUP * SLAB)
        return pltpu.make_async_copy(zero_ref, xs_ref.at[rows], zsem)

    @pl.when(i == 0)
    def _():
        zero_ref[...] = jnp.zeros(zero_ref.shape, F32)
        for z in range(2 * N_EXPERTS):
            @pl.when(zrow_ref[z] >= 0)
            def _():
                zero_fill(z).start()
        for z in range(2 * N_EXPERTS):
            @pl.when(zrow_ref[z] >= 0)
            def _():
                zero_fill(z).wait()

    base = i * TM_MOVE

    def issue(t, carry):
        _row_copy(hn_ref, t, xs_ref, d0_ref[base + t], sem).start(priority=0)
        _row_copy(hn_ref, t, xs_ref, d1_ref[base + t], sem).start(priority=1)
        return carry

    lax.fori_loop(0, TM_MOVE, issue, 0, unroll=8)

    def drain(t, carry):
        _row_copy(hn_ref, t, xs_ref, d0_ref[base + t], sem).wait()
        _row_copy(hn_ref, t, xs_ref, d1_ref[base + t], sem).wait()
        return carry

    lax.fori_loop(0, TM_MOVE, drain, 0, unroll=8)


def _dispatch(d0, d1, zero_rows, hn, rows):
    n = hn.shape[0] // SLAB
    return pl.pallas_call(
        _dispatch_kernel,
        grid_spec=pltpu.PrefetchScalarGridSpec(
            num_scalar_prefetch=3,
            grid=(n // TM_MOVE,),
            in_specs=[pl.BlockSpec((TM_MOVE * SLAB, LANES), lambda i, d0, d1, z: (i, 0))],
            out_specs=pl.BlockSpec(memory_space=pl.ANY),
            scratch_shapes=[
                pltpu.VMEM((TM_GROUP * SLAB, LANES), F32),
                pltpu.SemaphoreType.DMA,
                pltpu.SemaphoreType.DMA,
            ],
        ),
        out_shape=jax.ShapeDtypeStruct((rows * SLAB, LANES), F32),
        compiler_params=_params("arbitrary"),
        name="moe_dispatch",
    )(d0, d1, zero_rows, hn)


def _group_ffn_kernel(te_ref, tv_ref, x_ref, wg_ref, wu_ref, wd_ref, o_ref, xb_ref, acc_ref):
    g = pl.program_id(0)
    j = pl.program_id(1)

    @pl.when((tv_ref[g] == 0) & (j == 0))
    def _():
        o_ref[...] = jnp.zeros(o_ref.shape, F32)

    @pl.when(tv_ref[g] > 0)
    def _():
        @pl.when(j == 0)
        def _():
            for s in range(SLAB):
                xb_ref[:, s * LANES:(s + 1) * LANES] = _load_slab_cols(
                    x_ref, s, TM_GROUP).astype(BF16)
            acc_ref[...] = jnp.zeros(acc_ref.shape, F32)

        x = xb_ref[...]
        gate = jnp.dot(x, wg_ref[0], preferred_element_type=F32)
        up = jnp.dot(x, wu_ref[0], preferred_element_type=F32)
        act = (jax.nn.silu(gate) * up).astype(BF16)
        acc_ref[...] += jnp.dot(act, wd_ref[0], preferred_element_type=F32)

        @pl.when(j == pl.num_programs(1) - 1)
        def _():
            _store_slabs(o_ref, acc_ref[...])


def _group_ffn(tile_expert, tile_valid, xs, w_gate, w_up, w_down):
    rows = xs.shape[0] // SLAB
    n_chunks = FFN_EXPERT // FFN_EXPERT_CHUNK
    last = n_chunks - 1
    chunk = lambda g, j, tv: jnp.where(g % 2 == 0, j, last - j)
    row_map = lambda g, j, te, tv: (g, 0)
    return pl.pallas_call(
        _group_ffn_kernel,
        grid_spec=pltpu.PrefetchScalarGridSpec(
            num_scalar_prefetch=2,
            grid=(rows // TM_GROUP, n_chunks),
            in_specs=[
                pl.BlockSpec((TM_GROUP * SLAB, LANES), row_map),
                pl.BlockSpec((1, D_MODEL, FFN_EXPERT_CHUNK),
                             lambda g, j, te, tv: (te[g], 0, chunk(g, j, tv))),
                pl.BlockSpec((1, D_MODEL, FFN_EXPERT_CHUNK),
                             lambda g, j, te, tv: (te[g], 0, chunk(g, j, tv))),
                pl.BlockSpec((1, FFN_EXPERT_CHUNK, D_MODEL),
                             lambda g, j, te, tv: (te[g], chunk(g, j, tv), 0)),
            ],
            out_specs=pl.BlockSpec((TM_GROUP * SLAB, LANES), row_map),
            scratch_shapes=[
                pltpu.VMEM((TM_GROUP, D_MODEL), BF16),
                pltpu.VMEM((TM_GROUP, D_MODEL), F32),
            ],
        ),
        out_shape=jax.ShapeDtypeStruct((rows * SLAB, LANES), F32),
        compiler_params=_params("arbitrary", "arbitrary"),
        name="moe_group_ffn",
    )(tile_expert, tile_valid, xs, w_gate, w_up, w_down)


def _combine_kernel(d0_ref, d1_ref, h_ref, rw_ref, ys_ref, o_ref, r0_ref, r1_ref, sems):
    i = pl.program_id(0)
    slot = i % 2

    def gather(tile, buf, action):
        base = tile * TM_MOVE
        r0, r1, sem = r0_ref.at[buf], r1_ref.at[buf], sems.at[buf]

        def body(t, carry):
            c0 = _row_copy(ys_ref, d0_ref[base + t], r0, t, sem)
            c1 = _row_copy(ys_ref, d1_ref[base + t], r1, t, sem)
            if action == "start":
                c0.start(priority=0)
                c1.start(priority=1)
            else:
                c0.wait()
                c1.wait()
            return carry

        lax.fori_loop(0, TM_MOVE, body, 0, unroll=8)

    @pl.when(i == 0)
    def _():
        gather(0, 0, "start")

    @pl.when(i + 1 < pl.num_programs(0))
    def _():
        gather(i + 1, 1 - slot, "start")

    gather(i, slot, "wait")
    rw = rw_ref[...]
    w1 = rw[:, 0:1]
    w2 = rw[:, 1:2]
    r0, r1 = r0_ref.at[slot], r1_ref.at[slot]
    for s in range(SLAB):
        cols = slice(s * LANES, (s + 1) * LANES)
        o_ref[:, cols] = (h_ref[:, cols] + w1 * _load_slab_cols(r0, s, TM_MOVE)
                          + w2 * _load_slab_cols(r1, s, TM_MOVE))


def _combine(d0, d1, h, route_w, ys):
    n = h.shape[0]
    tok = lambda i, d0, d1: (i, 0)
    return pl.pallas_call(
        _combine_kernel,
        grid_spec=pltpu.PrefetchScalarGridSpec(
            num_scalar_prefetch=2,
            grid=(n // TM_MOVE,),
            in_specs=[
                pl.BlockSpec((TM_MOVE, D_MODEL), tok),
                pl.BlockSpec((TM_MOVE, LANES), tok),
                pl.BlockSpec(memory_space=pl.ANY),
            ],
            out_specs=pl.BlockSpec((TM_MOVE, D_MODEL), tok),
            scratch_shapes=[
                pltpu.VMEM((2, TM_MOVE * SLAB, LANES), F32),
                pltpu.VMEM((2, TM_MOVE * SLAB, LANES), F32),
                pltpu.SemaphoreType.DMA((2,)),
            ],
        ),
        out_shape=jax.ShapeDtypeStruct((n, D_MODEL), F32),
        compiler_params=_params("arbitrary"),
        name="moe_combine",
    )(d0, d1, h, route_w, ys)


def _routing_tables(route_i, counts, n):
    counts = counts.astype(I32)
    padded = (counts + TM_GROUP - 1) // TM_GROUP * TM_GROUP
    ends = jnp.cumsum(padded)
    starts = ends - padded
    d0 = starts[route_i[0]] + route_i[2]
    d1 = starts[route_i[1]] + route_i[3]
    n_tiles = TOP_K * n // TM_GROUP + N_EXPERTS
    used = ends[-1] // TM_GROUP
    g = jnp.arange(n_tiles, dtype=I32)
    first_row = jnp.minimum(g, used - 1) * TM_GROUP
    tile_expert = jnp.sum((ends[None, :] <= first_row[:, None]).astype(I32), axis=1)
    tile_valid = (g < used).astype(I32)
    tail = g[-N_EXPERTS:]
    zero_rows = jnp.concatenate([
        jnp.where(padded > 0, ends - TM_GROUP, -1),
        jnp.where(tail >= used, tail * TM_GROUP, -1)]).astype(I32)
    return d0, d1, zero_rows, tile_expert, tile_valid, n_tiles * TM_GROUP


def kernel(x, positions, l0_mix_norm, l0_sg_w_in, l0_sg_v_norm, l0_sg_w_spatial, l0_sg_b_spatial, l0_sg_w_out, l0_ffn_norm, l0_ffn_w_gate, l0_ffn_w_up, l0_ffn_w_down, l1_mix_norm, l1_da_w_qkv, l1_da_q_norm, l1_da_k_norm, l1_da_lambda_q1, l1_da_lambda_k1, l1_da_lambda_q2, l1_da_lambda_k2, l1_da_subln, l1_da_w_out, l1_moe_norm, l1_moe_w_router, l1_moe_w_gate, l1_moe_w_up, l1_moe_w_down):
    batch, seq, _ = x.shape
    n = batch * seq
    row = lambda a: a.reshape(1, -1)
    h = x.reshape(n, D_MODEL)

    h = _sg_mixer(h, row(l0_mix_norm), l0_sg_w_in.astype(BF16), row(l0_sg_v_norm),
                  l0_sg_w_spatial, l0_sg_b_spatial.T, l0_sg_w_out.astype(BF16))
    h = _dense_ffn(h, row(l0_ffn_norm), l0_ffn_w_gate.astype(BF16),
                   l0_ffn_w_up.astype(BF16), l0_ffn_w_down.astype(BF16))

    inv_freq = 1.0 / (ROPE_THETA ** (jnp.arange(0, ROT_DIM, 2, dtype=F32) / ROT_DIM))
    d = jnp.arange(HEAD_WIDTH) % DA_HEAD_DIM
    freq_lane = jnp.where(d < ROT_DIM, inv_freq[d % ROT_HALF], 0.0).reshape(1, HEAD_WIDTH)
    seg = jnp.arange(HEAD_WIDTH) // DA_HEAD_DIM
    seg_ones = (seg[:, None] == seg[None, :]).astype(BF16)
    q, k, vt = _qkv(h, positions.reshape(n, 1), row(l1_mix_norm), l1_da_w_qkv.astype(BF16),
                    freq_lane, row(jnp.tile(l1_da_q_norm, 2)), row(jnp.tile(l1_da_k_norm, 2)),
                    seg_ones, batch, seq)
    lam_terms = jnp.stack([l1_da_lambda_q1, l1_da_lambda_k1, l1_da_lambda_q2, l1_da_lambda_k2])
    o, (w_gate, w_up, w_down) = _attention(
        q, k, vt, lam_terms, row(l1_da_subln), batch, seq,
        (l1_moe_w_gate, l1_moe_w_up, l1_moe_w_down))

    w_router = jnp.pad(l1_moe_w_router, ((0, 0), (0, LANES - N_EXPERTS)))
    w_router_hi = w_router.astype(BF16)
    w_router_lo = (w_router - w_router_hi.astype(F32)).astype(BF16)
    h, hn, route_i, route_w, counts = _proj_route(
        o, h, l1_da_w_out.astype(BF16), row(l1_moe_norm),
        jnp.concatenate([w_router_hi, w_router_lo], axis=1))
    d0, d1, zero_rows, tile_expert, tile_valid, rows = _routing_tables(
        route_i, counts[0, :N_EXPERTS], n)
    xs = _dispatch(d0, d1, zero_rows, hn, rows)
    ys = _group_ffn(tile_expert, tile_valid, xs, w_gate, w_up, w_down)
    h = _combine(d0, d1, h, route_w, ys)
    return h.reshape(batch, seq, D_MODEL)
```
